```python
import math
import jax
import jax.numpy as jnp
from jax import lax
import numpy as np


D_MODEL = 1024
BATCH = 8
SEQ = 4096
DEPTH = 4

HEAD_DIM = 64
N_HEADS_GDN = 6
N_HEADS_DSW = 6
N_HEADS_SB = 4
D_GDN = N_HEADS_GDN * HEAD_DIM
D_DSW = N_HEADS_DSW * HEAD_DIM
D_SB = N_HEADS_SB * HEAD_DIM
D_MIX = D_GDN + D_DSW + D_SB
CONV_WIDTH = 4
GDN_CHUNK = 64
DSW_CONFIGS = ((128, 1), (512, 4), (2048, 16))
SB_BLOCK = 128
D_FF = 2816
N_EXPERTS = 8
TOP_K = 2
MOE_D_FF = 3584
PLE_DIM = 256
N_DENSE = (DEPTH + 1) // 2
N_MOE = DEPTH // 2
RMS_EPS = 1e-6
IN_SPLITS = (D_GDN,) * 4 + (N_HEADS_GDN,) * 2 + (D_DSW,) * 3 + (D_SB,) * 3
D_IN = 4 * D_GDN + 2 * N_HEADS_GDN + 3 * D_DSW + 3 * D_SB

kernel_name = 'hybrid_gdn_dilated_stickbreaking_moe'

F32 = jnp.float32


def rms_norm(x, w):
    xf = x.astype(F32)
    y = xf * lax.rsqrt(jnp.mean(xf * xf, axis=-1, keepdims=True) + RMS_EPS)
    return (y * w.astype(F32)).astype(x.dtype)


def l2_normalize(x):
    xf = x.astype(F32)
    return xf * lax.rsqrt(jnp.sum(xf * xf, axis=-1, keepdims=True) + RMS_EPS)


def to_heads(t, n_heads):
    b, s, _ = t.shape
    return t.reshape(b, s, n_heads, -1).transpose(0, 2, 1, 3)


def from_heads(t):
    b, h, s, d = t.shape
    return t.transpose(0, 2, 1, 3).reshape(b, s, h * d)


def causal_depthwise_conv(x, w):
    c = x.shape[-1]
    y = lax.conv_general_dilated(
        x, w.astype(x.dtype)[:, None, :], window_strides=(1,),
        padding=[(CONV_WIDTH - 1, 0)], dimension_numbers=('NWC', 'WIO', 'NWC'),
        feature_group_count=c)
    return jax.nn.silu(y)


def gated_delta_rule(q, k, v, g, beta):
    b, h, s, dk = q.shape
    dv = v.shape[-1]
    c = GDN_CHUNK
    n = s // c

    def chunks(t):
        t = t.astype(F32).reshape(b, h, n, c, *t.shape[3:])
        return jnp.moveaxis(t, 2, 0)

    q = chunks(q) * (dk ** -0.5)
    k, v, g, beta = chunks(k), chunks(v), chunks(g), chunks(beta)
    G = jnp.cumsum(g, axis=-1)
    idx = jnp.arange(c)
    causal = idx[:, None] >= idx[None, :]
    strict = idx[:, None] > idx[None, :]
    decay = jnp.exp(jnp.where(causal, G[..., :, None] - G[..., None, :], -jnp.inf))
    kk = jnp.einsum('nbhid,nbhjd->nbhij', k, k)
    tri = jnp.where(strict, beta[..., :, None] * kk * decay, 0.0) + jnp.eye(c, dtype=F32)
    rhs = jnp.concatenate([v * beta[..., None], k * (beta * jnp.exp(G))[..., None]], axis=-1)
    sol = lax.linalg.triangular_solve(tri, rhs, left_side=True, lower=True, unit_diagonal=True)
    u, w = sol[..., :dv], sol[..., dv:]
    qk = jnp.einsum('nbhid,nbhjd->nbhij', q, k) * decay
    q_dec = q * jnp.exp(G)[..., None]
    k_dec = k * jnp.exp(G[..., -1:] - G)[..., None]
    g_tot = jnp.exp(G[..., -1])[..., None, None]

    def step(state, xs):
        u_n, w_n, qk_n, q_n, k_n, gt_n = xs
        v_new = u_n - jnp.einsum('bhcd,bhde->bhce', w_n, state)
        o_n = jnp.einsum('bhcd,bhde->bhce', q_n, state) + jnp.einsum('bhij,bhje->bhie', qk_n, v_new)
        state = state * gt_n + jnp.einsum('bhcd,bhce->bhde', k_n, v_new)
        return state, o_n

    s0 = jnp.zeros((b, h, dk, dv), F32)
    _, o = lax.scan(step, s0, (u, w, qk, q_dec, k_dec, g_tot))
    return jnp.moveaxis(o, 0, 2).reshape(b, h, s, dv)


def gdn_mixer(q, k, v, z, a, bl, conv_w, a_log, dt_bias, norm_w):
    qkv = causal_depthwise_conv(jnp.concatenate([q, k, v], axis=-1), conv_w)
    q, k, v = jnp.split(qkv, 3, axis=-1)
    q = l2_normalize(to_heads(q, N_HEADS_GDN))
    k = l2_normalize(to_heads(k, N_HEADS_GDN))
    v = to_heads(v, N_HEADS_GDN)
    g = -jnp.exp(a_log.astype(F32)) * jax.nn.softplus(a.astype(F32) + dt_bias.astype(F32))
    beta = jax.nn.sigmoid(bl.astype(F32))
    o = gated_delta_rule(q, k, v, g.transpose(0, 2, 1), beta.transpose(0, 2, 1))
    o = from_heads(rms_norm(o, norm_w))
    return (o * jax.nn.silu(z.astype(F32))).astype(z.dtype)


def alibi_slopes(n):
    return 2.0 ** (-8.0 * (jnp.arange(n, dtype=F32) + 1.0) / n)


def dilated_window_attention(q, k, v, slopes, window, dilation):
    b, h, s, d = q.shape
    r = dilation
    span = window // dilation
    blk = span
    L = s // r
    nb = -(-L // blk)
    lp = nb * blk

    def sub(t):
        t = t.reshape(b, h, L, r, d).transpose(0, 1, 3, 2, 4)
        t = jnp.pad(t, ((0, 0), (0, 0), (0, 0), (0, lp - L), (0, 0)))
        return t.reshape(b, h, r, nb, blk, d)

    def with_prev(t):
        prev = jnp.pad(t[:, :, :, :-1], ((0, 0), (0, 0), (0, 0), (1, 0), (0, 0), (0, 0)))
        return jnp.concatenate([prev, t], axis=4)

    qb = sub(q)
    kc, vc = with_prev(sub(k)), with_prev(sub(v))
    scores = jnp.einsum('bhrnqd,bhrnkd->bhrnqk', qb, kc).astype(F32) * (d ** -0.5)
    qi = jnp.arange(blk)[:, None]
    ki = jnp.arange(2 * blk)[None, :]
    dist = qi - ki + blk
    n_idx = jnp.arange(nb)[:, None, None]
    valid = (dist >= 0) & (dist <= span) & (n_idx * blk + ki - blk >= 0)
    bias = -slopes[:, None, None, None, None] * (dist * r).astype(F32)
    scores = jnp.where(valid, scores + bias, -jnp.inf)
    m = jnp.max(scores, axis=-1, keepdims=True)
    e = jnp.exp(scores - m)
    den = jnp.sum(e, axis=-1, keepdims=True)
    o = jnp.einsum('bhrnqk,bhrnkd->bhrnqd', e, vc.astype(F32)) / den
    lse = (m + jnp.log(den))[..., 0]

    def unsub(t):
        t = t.reshape(b, h, r, lp, *t.shape[5:])[:, :, :, :L]
        t = jnp.moveaxis(t, 2, 3)
        return t.reshape(b, h, s, *t.shape[4:])

    return unsub(o), unsub(lse)


def dsw_mixer(q, k, v):
    slopes = alibi_slopes(N_HEADS_DSW)
    outs = []
    lses = []
    for window, dilation in DSW_CONFIGS:
        o_c, lse_c = dilated_window_attention(q, k, v, slopes, window, dilation)
        outs.append(o_c)
        lses.append(lse_c)
    wts = jax.nn.softmax(jnp.stack(lses, axis=0), axis=0)
    return jnp.sum(wts[..., None] * jnp.stack(outs, axis=0), axis=0)


def stick_breaking_attention(q, k, v):
    b, h, s, d = q.shape
    nb = s // SB_BLOCK
    kf = k.astype(F32)
    vf = v.astype(F32)
    qb = jnp.moveaxis(q.astype(F32).reshape(b, h, nb, SB_BLOCK, d), 2, 0) * (d ** -0.5)
    key_pos = jnp.arange(s)

    def block(args):
        q_blk, start = args
        z = jnp.einsum('bhqd,bhkd->bhqk', q_blk, kf)
        t = start + jnp.arange(SB_BLOCK)
        before = key_pos[None, :] < t[:, None]
        log_keep = jnp.where(before, jax.nn.log_sigmoid(-z), 0.0)
        suffix = lax.cumsum(log_keep, axis=3, reverse=True) - log_keep
        a = jnp.where(before, jnp.exp(jax.nn.log_sigmoid(z) + suffix), 0.0)
        return jnp.einsum('bhqk,bhkd->bhqd', a, vf)

    o = lax.map(block, (qb, jnp.arange(nb) * SB_BLOCK))
    return jnp.moveaxis(o, 0, 2).reshape(b, h, s, d)


def token_mixers(hn, w_in, conv_w, a_log, dt_bias, gdn_norm_w, dsw_norm_w, sb_norm_w, w_out):
    proj = hn @ w_in
    cuts = [int(c) for c in np.cumsum(IN_SPLITS)[:-1]]
    (gq, gk, gv, gz, ga, gb, dq, dk, dv, sq, sk, sv) = jnp.split(proj, cuts, axis=-1)
    o_gdn = gdn_mixer(gq, gk, gv, gz, ga, gb, conv_w, a_log, dt_bias, gdn_norm_w)
    o_dsw = dsw_mixer(to_heads(dq, N_HEADS_DSW), to_heads(dk, N_HEADS_DSW), to_heads(dv, N_HEADS_DSW))
    o_dsw = from_heads(rms_norm(o_dsw, dsw_norm_w))
    o_sb = stick_breaking_attention(to_heads(sq, N_HEADS_SB), to_heads(sk, N_HEADS_SB), to_heads(sv, N_HEADS_SB))
    o_sb = from_heads(rms_norm(o_sb, sb_norm_w))
    o = jnp.concatenate([o_gdn.astype(hn.dtype), o_dsw.astype(hn.dtype), o_sb.astype(hn.dtype)], axis=-1)
    return o @ w_out


def swiglu(x, w1, w3, w2):
    return (jax.nn.silu(x @ w1) * (x @ w3)) @ w2


def moe_swiglu(x, router_w, w1, w3, w2):
    logits = (x @ router_w).astype(F32)
    top_val, top_idx = lax.top_k(logits, TOP_K)
    top_w = jax.nn.softmax(top_val, axis=-1)
    gates = jnp.sum(jax.nn.one_hot(top_idx, N_EXPERTS, dtype=F32) * top_w[..., None], axis=-2)
    y = jnp.zeros(x.shape, F32)
    for e in range(N_EXPERTS):
        y = y + gates[..., e:e + 1] * swiglu(x, w1[e], w3[e], w2[e]).astype(F32)
    return y.astype(x.dtype)


def setup_inputs(seed: int = 0) -> dict:
    key = jax.random.key(seed)
    ks = jax.random.split(key, 24)

    def nrm(k, shape, scale):
        return scale * jax.random.normal(k, shape, F32)

    dt = jnp.exp(jax.random.uniform(ks[6], (DEPTH, N_HEADS_GDN), F32, math.log(1e-3), math.log(1e-1)))
    return {
        'x': nrm(ks[0], (BATCH, SEQ, D_MODEL), 1.0),
        'p': nrm(ks[1], (DEPTH, BATCH, SEQ, PLE_DIM), 1.0),
        'attn_norm_w': 1.0 + nrm(ks[2], (DEPTH, D_MODEL), 0.02),
        'w_in': nrm(ks[3], (DEPTH, D_MODEL, D_IN), D_MODEL ** -0.5),
        'conv_w': nrm(ks[4], (DEPTH, CONV_WIDTH, 3 * D_GDN), CONV_WIDTH ** -0.5),
        'a_log': jnp.log(jax.random.uniform(ks[5], (DEPTH, N_HEADS_GDN), F32, 1.0, 16.0)),
        'dt_bias': dt + jnp.log(-jnp.expm1(-dt)),
        'gdn_norm_w': 1.0 + nrm(ks[7], (DEPTH, HEAD_DIM), 0.02),
        'dsw_norm_w': 1.0 + nrm(ks[8], (DEPTH, HEAD_DIM), 0.02),
        'sb_norm_w': 1.0 + nrm(ks[9], (DEPTH, HEAD_DIM), 0.02),
        'w_out': nrm(ks[10], (DEPTH, D_MIX, D_MODEL), D_MIX ** -0.5),
        'ffn_norm_w': 1.0 + nrm(ks[11], (DEPTH, D_MODEL), 0.02),
        'ffn_w1': nrm(ks[12], (N_DENSE, D_MODEL, D_FF), D_MODEL ** -0.5),
        'ffn_w3': nrm(ks[13], (N_DENSE, D_MODEL, D_FF), D_MODEL ** -0.5),
        'ffn_w2': nrm(ks[14], (N_DENSE, D_FF, D_MODEL), D_FF ** -0.5),
        'router_w': nrm(ks[15], (N_MOE, D_MODEL, N_EXPERTS), D_MODEL ** -0.5),
        'moe_w1': nrm(ks[16], (N_MOE, N_EXPERTS, D_MODEL, MOE_D_FF), D_MODEL ** -0.5),
        'moe_w3': nrm(ks[17], (N_MOE, N_EXPERTS, D_MODEL, MOE_D_FF), D_MODEL ** -0.5),
        'moe_w2': nrm(ks[18], (N_MOE, N_EXPERTS, MOE_D_FF, D_MODEL), MOE_D_FF ** -0.5),
        'ple_proj': nrm(ks[19], (DEPTH, PLE_DIM, D_MODEL), PLE_DIM ** -0.5),
        'ple_norm_w': 1.0 + nrm(ks[20], (DEPTH, D_MODEL), 0.02),
        'ple_gate': nrm(ks[21], (DEPTH, D_MODEL, D_MODEL), D_MODEL ** -0.5),
        'final_norm_w': 1.0 + nrm(ks[22], (D_MODEL,), 0.02),
    }


def reference(x, p, attn_norm_w, w_in, conv_w, a_log, dt_bias, gdn_norm_w, dsw_norm_w,
              sb_norm_w, w_out, ffn_norm_w, ffn_w1, ffn_w3, ffn_w2, router_w, moe_w1,
              moe_w3, moe_w2, ple_proj, ple_norm_w, ple_gate, final_norm_w):
    h = x
    for i in range(DEPTH):
        hn = rms_norm(h, attn_norm_w[i])
        h = h + token_mixers(hn, w_in[i], conv_w[i], a_log[i], dt_bias[i], gdn_norm_w[i],
                             dsw_norm_w[i], sb_norm_w[i], w_out[i])
        hn = rms_norm(h, ffn_norm_w[i])
        j = i // 2
        if i % 2 == 0:
            h = h + swiglu(hn, ffn_w1[j], ffn_w3[j], ffn_w2[j])
        else:
            h = h + moe_swiglu(hn, router_w[j], moe_w1[j], moe_w3[j], moe_w2[j])
        gate = jax.nn.sigmoid(rms_norm(h, ple_norm_w[i]) @ ple_gate[i])
        h = h + (p[i] @ ple_proj[i]) * gate
    return rms_norm(h, final_norm_w)
```

```python
import functools
import math

import jax
import jax.numpy as jnp
import numpy as np
from jax import lax
from jax.experimental import pallas as pl
from jax.experimental.pallas import tpu as pltpu

F32 = jnp.float32
BF16 = jnp.bfloat16

D_MODEL = 1024
HEAD_DIM = 64
LANES = 128
N_PAIRS_GDN = 3
N_PAIRS_DSW = 3
N_PAIRS_SB = 2
D_GDN = 384
D_DSW = 384
D_SB = 256
CONV_WIDTH = 4
GDN_CHUNK = 64
DSW_CONFIGS = ((128, 1), (512, 4), (2048, 16))
DSW_SPAN = 128
D_FF = 2816
N_EXPERTS = 8
MOE_D_FF = 3584
PLE_DIM = 256
RMS_EPS = 1e-6
NEG_BIG = -1e30

COL_GQ, COL_GK, COL_GV, COL_GZ = 0, 384, 768, 1152
COL_DQ, COL_DK, COL_DV = 1536, 1920, 2304
COL_SQ, COL_SK, COL_SV = 2688, 2944, 3200
COL_AB = 3456
D_PROJ = 3584

VMEM_LIMIT_BYTES = 56 * 1024 * 1024


def _cparams(sem, vmem=None):
    return pltpu.CompilerParams(dimension_semantics=sem, vmem_limit_bytes=vmem)


def _rms(x, w):
    ms = jnp.mean(x * x, axis=-1, keepdims=True)
    return x * lax.rsqrt(ms + RMS_EPS) * w


def _proj_kernel(h_ref, nw_ref, w_ref, o_ref, *, tn):
    hn = _rms(h_ref[...], nw_ref[...]).astype(BF16)
    n = w_ref.shape[1]
    for c in range(n // tn):
        o_ref[:, c * tn:(c + 1) * tn] = jnp.dot(
            hn, w_ref[:, c * tn:(c + 1) * tn], preferred_element_type=F32).astype(o_ref.dtype)


def norm_proj(h, nw, w, *, tm=512, tn=512):
    t, d = h.shape
    n = w.shape[1]
    return pl.pallas_call(
        functools.partial(_proj_kernel, tn=tn),
        grid=(t // tm,),
        in_specs=[pl.BlockSpec((tm, d), lambda i: (i, 0)),
                  pl.BlockSpec((1, d), lambda i: (0, 0)),
                  pl.BlockSpec((d, n), lambda i: (0, 0))],
        out_specs=pl.BlockSpec((tm, n), lambda i: (i, 0)),
        out_shape=jax.ShapeDtypeStruct((t, n), BF16),
        compiler_params=_cparams(("arbitrary",), VMEM_LIMIT_BYTES),
        name="norm_proj",
    )(h, nw.reshape(1, d), w)


def _mix_out_kernel(h_ref, a_ref, b_ref, c_ref, w_ref, o_ref):
    acc = jnp.dot(a_ref[...], w_ref[0:D_GDN, :], preferred_element_type=F32)
    acc += jnp.dot(b_ref[...], w_ref[D_GDN:D_GDN + D_DSW, :], preferred_element_type=F32)
    acc += jnp.dot(c_ref[...], w_ref[D_GDN + D_DSW:, :], preferred_element_type=F32)
    o_ref[...] = h_ref[...] + acc


def mix_out(h, o_gdn, o_dsw, o_sb, w_out, *, tm=512):
    t, d = h.shape
    return pl.pallas_call(
        _mix_out_kernel,
        grid=(t // tm,),
        in_specs=[pl.BlockSpec((tm, d), lambda i: (i, 0)),
                  pl.BlockSpec((tm, D_GDN), lambda i: (i, 0)),
                  pl.BlockSpec((tm, D_DSW), lambda i: (i, 0)),
                  pl.BlockSpec((tm, D_SB), lambda i: (i, 0)),
                  pl.BlockSpec((d, d), lambda i: (0, 0))],
        out_specs=pl.BlockSpec((tm, d), lambda i: (i, 0)),
        out_shape=jax.ShapeDtypeStruct((t, d), F32),
        compiler_params=_cparams(("arbitrary",), VMEM_LIMIT_BYTES),
        name="mix_out",
    )(h, o_gdn, o_dsw, o_sb, w_out)


def _ffn_kernel(h_ref, nw_ref, w1_ref, w3_ref, w2_ref, o_ref, *, tf):
    h = h_ref[...]
    hn = _rms(h, nw_ref[...]).astype(BF16)
    acc = h
    for c in range(w1_ref.shape[1] // tf):
        sl = slice(c * tf, (c + 1) * tf)
        a = jnp.dot(hn, w1_ref[:, sl], preferred_element_type=F32)
        b = jnp.dot(hn, w3_ref[:, sl], preferred_element_type=F32)
        g = (a * jax.nn.sigmoid(a) * b).astype(BF16)
        acc = acc + jnp.dot(g, w2_ref[sl, :], preferred_element_type=F32)
    o_ref[...] = acc


def ffn_dense(h, nw, w1, w3, w2, *, tm=512, tf=704):
    t, d = h.shape
    f = w1.shape[1]
    const = dict(pipeline_mode=pl.Buffered(1))
    return pl.pallas_call(
        functools.partial(_ffn_kernel, tf=tf),
        grid=(t // tm,),
        in_specs=[pl.BlockSpec((tm, d), lambda i: (i, 0)),
                  pl.BlockSpec((1, d), lambda i: (0, 0)),
                  pl.BlockSpec((d, f), lambda i: (0, 0), **const),
                  pl.BlockSpec((d, f), lambda i: (0, 0), **const),
                  pl.BlockSpec((f, d), lambda i: (0, 0), **const)],
        out_specs=pl.BlockSpec((tm, d), lambda i: (i, 0)),
        out_shape=jax.ShapeDtypeStruct((t, d), F32),
        compiler_params=_cparams(("arbitrary",), VMEM_LIMIT_BYTES),
        name="ffn_dense",
    )(h, nw.reshape(1, d), w1, w3, w2)


def _ple_kernel(*refs, n_extra, final):
    h_ref = refs[0]
    extra = refs[1:1 + n_extra]
    p_ref, proj_ref, nw_ref, gate_ref, fw_ref, o_ref = refs[1 + n_extra:]
    h = h_ref[...]
    for e in extra:
        h = h + e[0]
    hn = _rms(h, nw_ref[...]).astype(BF16)
    gate = jax.nn.sigmoid(jnp.dot(hn, gate_ref[...], preferred_element_type=F32))
    emb = jnp.dot(p_ref[...].astype(BF16), proj_ref[...], preferred_element_type=F32)
    out = h + emb * gate
    if final:
        out = _rms(out, fw_ref[...])
    o_ref[...] = out


def ple_layer(h, extras, p, proj, nw, gate, final_w, *, final, tm=512):
    t, d = h.shape
    in_specs = [pl.BlockSpec((tm, d), lambda i: (i, 0))]
    args = [h]
    for arr, slot in extras:
        in_specs.append(pl.BlockSpec((1, tm, d), lambda i, s=slot: (s, i, 0)))
        args.append(arr)
    in_specs += [pl.BlockSpec((tm, PLE_DIM), lambda i: (i, 0)),
                 pl.BlockSpec((PLE_DIM, d), lambda i: (0, 0)),
                 pl.BlockSpec((1, d), lambda i: (0, 0)),
                 pl.BlockSpec((d, d), lambda i: (0, 0)),
                 pl.BlockSpec((1, d), lambda i: (0, 0))]
    args += [p, proj, nw.reshape(1, d), gate, final_w.reshape(1, d)]
    return pl.pallas_call(
        functools.partial(_ple_kernel, n_extra=len(extras), final=final),
        grid=(t // tm,),
        in_specs=in_specs,
        out_specs=pl.BlockSpec((tm, d), lambda i: (i, 0)),
        out_shape=jax.ShapeDtypeStruct((t, d), F32),
        compiler_params=_cparams(("arbitrary",), VMEM_LIMIT_BYTES),
        name="ple_layer",
    )(*args)


def _lane_iota(shape):
    return lax.broadcasted_iota(jnp.int32, shape, len(shape) - 1)


def _head_sumsq(x):
    lo = _lane_iota(x.shape) < HEAD_DIM
    sq = x * x
    s_lo = jnp.sum(jnp.where(lo, sq, 0.0), axis=-1, keepdims=True)
    s_hi = jnp.sum(jnp.where(lo, 0.0, sq), axis=-1, keepdims=True)
    return jnp.where(lo, s_lo, s_hi)


def _head_rms(x, w2):
    return x * lax.rsqrt(_head_sumsq(x) * (1.0 / HEAD_DIM) + RMS_EPS) * w2


def _stack_heads(x):
    lo = _lane_iota(x.shape) < HEAD_DIM
    zero = jnp.zeros_like(x)
    return jnp.concatenate([jnp.where(lo, x, zero), jnp.where(lo, zero, x)], axis=0)


def _dot(a, b):
    return jnp.dot(a, b, preferred_element_type=F32)


def _dot_nt(a, b):
    return lax.dot_general(a, b, (((1,), (1,)), ((), ())), preferred_element_type=F32)


def _dot_tn(a, b):
    return lax.dot_general(a, b, (((0,), (0,)), ((), ())), preferred_element_type=F32)


def _sb_kernel(q_ref, k_ref, v_ref, nw_ref, o_ref, *, tq):
    i = pl.program_id(2)
    lane = _lane_iota((1, LANES))
    row = lax.broadcasted_iota(jnp.int32, (tq, tq), 0)
    col = lax.broadcasted_iota(jnp.int32, (tq, tq), 1)
    upper = jnp.where(row > col, 1.0, 0.0).astype(BF16)
    diag_before = col < row
    q = q_ref[0] * jnp.asarray(HEAD_DIM ** -0.5, BF16)
    zero = jnp.zeros_like(q)
    qh = (jnp.where(lane < HEAD_DIM, q, zero), jnp.where(lane < HEAD_DIM, zero, q))

    def body(jj, carry):
        j = i - jj
        start = pl.multiple_of(j * tq, tq)
        kj = k_ref[0, pl.ds(start, tq), :]
        vj = v_ref[0, pl.ds(start, tq), :]
        before = jnp.logical_or(diag_before, jj > 0)
        out = []
        for hh in range(2):
            acc, c = carry[hh]
            z = _dot_nt(qh[hh], kj)
            sp = jnp.maximum(z, 0.0) + jnp.log(1.0 + jnp.exp(-jnp.abs(z)))
            lk = jnp.where(before, -sp, 0.0)
            lk_hi = lk.astype(BF16)
            lk_lo = (lk - lk_hi.astype(F32)).astype(BF16)
            suffix = _dot(lk_hi, upper) + _dot(lk_lo, upper)
            a = jnp.where(before, jnp.exp(z - sp + suffix + c), 0.0)
            acc = acc + _dot(a.astype(BF16), vj)
            c = c + jnp.sum(lk, axis=-1, keepdims=True)
            out.append((acc, c))
        return tuple(out)

    init = tuple((jnp.zeros((tq, LANES), F32), jnp.zeros((tq, 1), F32)) for _ in range(2))
    (acc0, _), (acc1, _) = lax.fori_loop(0, i + 1, body, init)
    o = jnp.where(lane < HEAD_DIM, acc0, acc1)
    o_ref[0] = _head_rms(o, nw_ref[...]).astype(o_ref.dtype)


def sb_attention(proj, norm_w, *, tq=256):
    b, s, _ = proj.shape
    cq, ck, cv = COL_SQ // LANES, COL_SK // LANES, COL_SV // LANES
    nw2 = jnp.tile(norm_w.astype(F32), 2).reshape(1, LANES)
    return pl.pallas_call(
        functools.partial(_sb_kernel, tq=tq),
        grid=(b, N_PAIRS_SB, s // tq),
        in_specs=[pl.BlockSpec((1, tq, LANES), lambda bi, p, i: (bi, i, cq + p)),
                  pl.BlockSpec((1, s, LANES), lambda bi, p, i: (bi, 0, ck + p)),
                  pl.BlockSpec((1, s, LANES), lambda bi, p, i: (bi, 0, cv + p)),
                  pl.BlockSpec((1, LANES), lambda bi, p, i: (0, 0))],
        out_specs=pl.BlockSpec((1, tq, LANES), lambda bi, p, i: (bi, i, p)),
        out_shape=jax.ShapeDtypeStruct((b, s, D_SB), BF16),
        compiler_params=_cparams(("arbitrary",) * 3, VMEM_LIMIT_BYTES),
        name="sb_attention",
    )(proj, proj, proj, nw2)


def _dsw_kernel(slope_ref, q_ref, k_ref, v_ref, dist_ref, o_ref, lse_ref, *, r, tq):
    hp = pl.program_id(1) % N_PAIRS_DSW
    length = q_ref.shape[1]
    lane = _lane_iota((1, LANES))
    lo = lane < HEAD_DIM

    def body(n, _):
        start = pl.multiple_of(n * tq, tq)
        pstart = pl.multiple_of(jnp.maximum(start - DSW_SPAN, 0), DSW_SPAN)
        q = q_ref[0, pl.ds(start, tq), :] * jnp.asarray(HEAD_DIM ** -0.5, BF16)
        kc = jnp.concatenate([k_ref[0, pl.ds(pstart, DSW_SPAN), :], k_ref[0, pl.ds(start, tq), :]], axis=0)
        vc = jnp.concatenate([v_ref[0, pl.ds(pstart, DSW_SPAN), :], v_ref[0, pl.ds(start, tq), :]], axis=0)
        dist = dist_ref[jnp.minimum(n, 1)]
        zero = jnp.zeros_like(q)
        os, ls = [], []
        for hh in range(2):
            qh = jnp.where(lo, q, zero) if hh == 0 else jnp.where(lo, zero, q)
            slope = slope_ref[2 * hp + hh] * float(r)
            sc = _dot_nt(qh, kc) - slope * dist
            m = jnp.max(sc, axis=-1, keepdims=True)
            e = jnp.exp(sc - m)
            den = jnp.sum(e, axis=-1, keepdims=True)
            os.append(_dot(e.astype(BF16), vc) / den)
            ls.append(m + jnp.log(den))
        o_ref[0, pl.ds(start, tq), :] = jnp.where(lo, os[0], os[1])
        lse_ref[0, pl.ds(start, tq), :] = jnp.where(lo, ls[0], ls[1])
        return 0

    lax.fori_loop(0, length // tq, body, 0)


def _dsw_dist_table(tq):
    qi = np.arange(tq)[:, None]
    ki = np.arange(tq + DSW_SPAN)[None, :]
    dist = qi - ki + DSW_SPAN
    valid = (dist >= 0) & (dist <= DSW_SPAN)
    later = np.where(valid, dist, 1e30).astype(np.float32)
    first = np.where(valid & (ki >= DSW_SPAN), dist, 1e30).astype(np.float32)
    return jnp.asarray(np.stack([first, later]))


def dsw_config(proj, slopes, *, r, tq=256):
    b, s, dp = proj.shape
    length = s // r
    tq = min(tq, length)
    nblk = dp // LANES
    cq, ck, cv = COL_DQ // LANES, COL_DK // LANES, COL_DV // LANES
    pr = proj.reshape(b, length, r * dp)
    dist = _dsw_dist_table(tq)

    def col(base):
        return lambda bi, g: (bi, 0, (g // N_PAIRS_DSW) * nblk + base + g % N_PAIRS_DSW)

    out_map = lambda bi, g: (bi, 0, g)
    o, lse = pl.pallas_call(
        functools.partial(_dsw_kernel, r=r, tq=tq),
        grid=(b, r * N_PAIRS_DSW),
        in_specs=[pl.BlockSpec(memory_space=pltpu.SMEM),
                  pl.BlockSpec((1, length, LANES), col(cq)),
                  pl.BlockSpec((1, length, LANES), col(ck)),
                  pl.BlockSpec((1, length, LANES), col(cv)),
                  pl.BlockSpec((2, tq, tq + DSW_SPAN), lambda bi, g: (0, 0, 0))],
        out_specs=[pl.BlockSpec((1, length, LANES), out_map),
                   pl.BlockSpec((1, length, LANES), out_map)],
        out_shape=[jax.ShapeDtypeStruct((b, length, r * D_DSW), F32)] * 2,
        compiler_params=_cparams(("arbitrary",) * 2, VMEM_LIMIT_BYTES),
        name=f"dsw_r{r}",
    )(slopes, pr, pr, pr, dist)
    return o.reshape(b * s, D_DSW), lse.reshape(b * s, D_DSW)


def _dsw_combine_kernel(o0, o1, o2, l0, l1, l2, nw_ref, out_ref):
    ls = (l0[...], l1[...], l2[...])
    m = jnp.maximum(jnp.maximum(ls[0], ls[1]), ls[2])
    ws = [jnp.exp(l - m) for l in ls]
    o = (ws[0] * o0[...] + ws[1] * o1[...] + ws[2] * o2[...]) / (ws[0] + ws[1] + ws[2])
    for p in range(N_PAIRS_DSW):
        sl = slice(p * LANES, (p + 1) * LANES)
        out_ref[:, sl] = _head_rms(o[:, sl], nw_ref[...]).astype(out_ref.dtype)


def dsw_combine(outs, lses, norm_w, *, tm=1024):
    t = outs[0].shape[0]
    nw2 = jnp.tile(norm_w.astype(F32), 2).reshape(1, LANES)
    spec = pl.BlockSpec((tm, D_DSW), lambda i: (i, 0))
    return pl.pallas_call(
        _dsw_combine_kernel,
        grid=(t // tm,),
        in_specs=[spec] * 6 + [pl.BlockSpec((1, LANES), lambda i: (0, 0))],
        out_specs=spec,
        out_shape=jax.ShapeDtypeStruct((t, D_DSW), BF16),
        compiler_params=_cparams(("arbitrary",), VMEM_LIMIT_BYTES),
        name="dsw_combine",
    )(*outs, *lses, nw2)


def _gdn_kernel(xq_ref, xk_ref, xv_ref, z_ref, ab_ref, cwq_ref, cwk_ref, cwv_ref,
                expand_ref, alog_ref, dtb_ref, nw_ref, o_ref):
    c = GDN_CHUNK
    seq = xq_ref.shape[1]
    lane = _lane_iota((c, LANES))
    rowi = lax.broadcasted_iota(jnp.int32, (c, LANES), 0)
    colj = jnp.where(lane < HEAD_DIM, lane, lane - HEAD_DIM)
    lo = lane < HEAD_DIM
    causal = rowi >= colj
    strict = rowi > colj
    eye_cat = jnp.where(rowi == colj, 1.0, 0.0)
    r2 = lax.broadcasted_iota(jnp.int32, (LANES, LANES), 0)
    c2 = lax.broadcasted_iota(jnp.int32, (LANES, LANES), 1)
    same_head = (r2 < HEAD_DIM) == (c2 < HEAD_DIM)
    cw = (cwq_ref[...], cwk_ref[...], cwv_ref[...])
    neg_exp_alog = -jnp.exp(alog_ref[...])
    dtb = dtb_ref[...]
    expand = expand_ref[0]
    nw2 = nw_ref[...]

    def conv_silu(x_ref, w, tail, start):
        x = x_ref[0, pl.ds(start, c), :].astype(F32)
        xx = jnp.concatenate([tail, x], axis=0)
        y = xx[8:] * w[3:4]
        for s in (1, 2, 3):
            y = y + pltpu.roll(xx, s, axis=0)[8:] * w[3 - s:4 - s]
        return y * jax.nn.sigmoid(y), x[c - 8:]

    def body(n, carry):
        state, tq_, tk_, tv_ = carry
        start = pl.multiple_of(n * c, c)
        q, tq_ = conv_silu(xq_ref, cw[0], tq_, start)
        k, tk_ = conv_silu(xk_ref, cw[1], tk_, start)
        v, tv_ = conv_silu(xv_ref, cw[2], tv_, start)
        q = q * lax.rsqrt(_head_sumsq(q) + RMS_EPS) * (HEAD_DIM ** -0.5)
        k = k * lax.rsqrt(_head_sumsq(k) + RMS_EPS)
        ab = _dot(ab_ref[0, pl.ds(start, c), :], expand)
        a_e, b_e = ab[:, :LANES], ab[:, LANES:]
        xg = a_e + dtb
        g = neg_exp_alog * (jnp.maximum(xg, 0.0) + jnp.log(1.0 + jnp.exp(-jnp.abs(xg))))
        beta = jax.nn.sigmoid(b_e)
        gcol = g
        for s in (1, 2, 4, 8, 16, 32):
            gcol = gcol + jnp.where(rowi >= s, pltpu.roll(gcol, s, axis=0), 0.0)
        grow = jnp.sum(jnp.where(rowi == colj, gcol, 0.0), axis=0, keepdims=True)
        glast = gcol[c - 1:c, :]
        diff = gcol - grow
        decay = jnp.exp(jnp.where(causal, diff, NEG_BIG))
        eg = jnp.exp(gcol)
        kb = k.astype(BF16)
        ks = _stack_heads(kb)
        kq = _dot_nt(jnp.concatenate([kb, q.astype(BF16)], axis=0), ks)
        kk_cat, qk_cat = kq[:c], kq[c:]
        p = jnp.where(strict, -(beta * kk_cat * decay), 0.0)
        t = eye_cat + p
        pb = p.astype(BF16)
        p = _dot(pb, _stack_heads(pb))
        for _ in range(4):
            pb = p.astype(BF16)
            res = _dot(jnp.concatenate([t.astype(BF16), pb], axis=0), _stack_heads(pb))
            t = t + res[:c]
            p = res[c:]
        pb = p.astype(BF16)
        t = t + _dot(t.astype(BF16), _stack_heads(pb))
        rhs = jnp.concatenate([_stack_heads((v * beta).astype(BF16)),
                               _stack_heads((k * beta * eg).astype(BF16))], axis=1)
        uw = _dot(t.astype(BF16), rhs)
        u, w = uw[:, :LANES], uw[:, LANES:]
        q_dec = q * eg
        k_dec = k * jnp.exp(glast - gcol)
        sb = state.astype(BF16)
        ws = _dot(jnp.concatenate([w.astype(BF16), q_dec.astype(BF16)], axis=0), sb)
        v_new = u - ws[:c]
        vb = v_new.astype(BF16)
        qk = jnp.where(causal, qk_cat * decay, 0.0)
        o = ws[c:] + _dot(qk.astype(BF16), _stack_heads(vb))
        upd = _dot_tn(k_dec.astype(BF16), vb)
        state = state * jnp.exp(glast) + jnp.where(same_head, upd, 0.0)
        zg = z_ref[0, pl.ds(start, c), :].astype(F32)
        out = _head_rms(o, nw2) * (zg * jax.nn.sigmoid(zg))
        o_ref[0, pl.ds(start, c), :] = out.astype(o_ref.dtype)
        return state, tq_, tk_, tv_

    tail0 = jnp.zeros((8, LANES), F32)
    lax.fori_loop(0, seq // c, body, (jnp.zeros((LANES, LANES), F32), tail0, tail0, tail0))


def gdn_mixer(proj, conv_w, a_log, dt_bias, norm_w):
    b, s, _ = proj.shape
    npair = N_PAIRS_GDN
    ex = np.zeros((npair, LANES, 2 * LANES), np.float32)
    for p in range(npair):
        for hh in range(2):
            ex[p, 2 * p + hh, hh * HEAD_DIM:(hh + 1) * HEAD_DIM] = 1.0
            ex[p, 2 * npair + 2 * p + hh, LANES + hh * HEAD_DIM:LANES + (hh + 1) * HEAD_DIM] = 1.0
    expand = jnp.asarray(ex, BF16)
    alog2 = jnp.repeat(a_log.astype(F32).reshape(npair, 1, 2), HEAD_DIM, axis=2)
    dtb2 = jnp.repeat(dt_bias.astype(F32).reshape(npair, 1, 2), HEAD_DIM, axis=2)
    nw2 = jnp.tile(norm_w.astype(F32), 2).reshape(1, LANES)
    cw = conv_w.astype(F32)

    def xcol(base):
        return pl.BlockSpec((1, s, LANES), lambda bi, p: (bi, 0, base // LANES + p))

    def wcol(base):
        return pl.BlockSpec((CONV_WIDTH, LANES), lambda bi, p: (0, base // LANES + p))

    return pl.pallas_call(
        _gdn_kernel,
        grid=(b, npair),
        in_specs=[xcol(COL_GQ), xcol(COL_GK), xcol(COL_GV), xcol(COL_GZ),
                  pl.BlockSpec((1, s, LANES), lambda bi, p: (bi, 0, COL_AB // LANES)),
                  wcol(0), wcol(D_GDN), wcol(2 * D_GDN),
                  pl.BlockSpec((1, LANES, 2 * LANES), lambda bi, p: (p, 0, 0)),
                  pl.BlockSpec((None, 1, LANES), lambda bi, p: (p, 0, 0)),
                  pl.BlockSpec((None, 1, LANES), lambda bi, p: (p, 0, 0)),
                  pl.BlockSpec((1, LANES), lambda bi, p: (0, 0))],
        out_specs=pl.BlockSpec((1, s, LANES), lambda bi, p: (bi, 0, p)),
        out_shape=jax.ShapeDtypeStruct((b, s, D_GDN), BF16),
        compiler_params=_cparams(("arbitrary",) * 2, VMEM_LIMIT_BYTES),
        name="gdn_mixer",
    )(proj, proj, proj, proj, proj, cw, cw, cw, expand, alog2, dtb2, nw2)


def _router_kernel(h_ref, nw_ref, rw_ref, hn_ref, idx_ref, gate_ref):
    hn = _rms(h_ref[...], nw_ref[...])
    hn_ref[...] = hn
    lt = lax.dot_general(rw_ref[...], hn, (((1,), (1,)), ((), ())),
                         preferred_element_type=F32, precision=lax.Precision.HIGHEST)
    row = lax.broadcasted_iota(jnp.int32, lt.shape, 0)
    v1 = jnp.max(lt, axis=0, keepdims=True)
    i1 = jnp.min(jnp.where(lt == v1, row, N_EXPERTS), axis=0, keepdims=True)
    lt2 = jnp.where(row == i1, -jnp.inf, lt)
    v2 = jnp.max(lt2, axis=0, keepdims=True)
    i2 = jnp.min(jnp.where(lt2 == v2, row, N_EXPERTS), axis=0, keepdims=True)
    e = jnp.exp(v2 - v1)
    g1 = 1.0 / (1.0 + e)
    idx_ref[...] = jnp.concatenate([i1, i2], axis=0)
    gate_ref[...] = jnp.concatenate([g1, e * g1], axis=0)


def moe_router(h, nw, router_w, *, tm=512):
    t, d = h.shape
    return pl.pallas_call(
        _router_kernel,
        grid=(t // tm,),
        in_specs=[pl.BlockSpec((tm, d), lambda i: (i, 0)),
                  pl.BlockSpec((1, d), lambda i: (0, 0)),
                  pl.BlockSpec((N_EXPERTS, d), lambda i: (0, 0))],
        out_specs=[pl.BlockSpec((tm, d), lambda i: (i, 0)),
                   pl.BlockSpec((2, tm), lambda i: (0, i)),
                   pl.BlockSpec((2, tm), lambda i: (0, i))],
        out_shape=[jax.ShapeDtypeStruct((t, d), F32),
                   jax.ShapeDtypeStruct((2, t), jnp.int32),
                   jax.ShapeDtypeStruct((2, t), F32)],
        compiler_params=_cparams(("arbitrary",), VMEM_LIMIT_BYTES),
        name="moe_router",
    )(h, nw.reshape(1, d), router_w.astype(F32).T)


def _moe_kernel(te_ref, tv_ref, tok_ref, dst_ref, gate_ref, hn_hbm, w1_ref, w3_ref, w2_ref, y_hbm,
                xbuf, xb, acc, ybuf, gsem, ssem, *, tm, nf):
    i = pl.program_id(0)
    f = pl.program_id(1)
    valid = tv_ref[i] > 0

    def row_in(r):
        return pltpu.make_async_copy(hn_hbm.at[pl.ds(tok_ref[0, 0, r], 1)], xbuf.at[pl.ds(r, 1)], gsem)

    def row_out(r):
        return pltpu.make_async_copy(ybuf.at[pl.ds(r, 1)], y_hbm.at[pl.ds(dst_ref[0, 0, r], 1)], ssem)

    @pl.when(jnp.logical_and(valid, f == 0))
    def _():
        def start(r, _):
            row_in(r).start()
            return 0
        lax.fori_loop(0, tm, start, 0)

        def wait(r, _):
            row_in(r).wait()
            return 0
        lax.fori_loop(0, tm, wait, 0)
        xb[...] = xbuf[...].astype(BF16)
        acc[...] = jnp.zeros_like(acc)

    @pl.when(valid)
    def _():
        x = xb[...]
        a = _dot(x, w1_ref[0])
        b = _dot(x, w3_ref[0])
        g = (a * jax.nn.sigmoid(a) * b).astype(BF16)
        acc[...] += _dot(g, w2_ref[0])

    @pl.when(jnp.logical_and(valid, f == nf - 1))
    def _():
        ybuf[...] = acc[...] * gate_ref[:, 0:1]

        def start(r, _):
            @pl.when(dst_ref[0, 0, r] >= 0)
            def _():
                row_out(r).start()
            return 0
        lax.fori_loop(0, tm, start, 0)

        def wait(r, _):
            @pl.when(dst_ref[0, 0, r] >= 0)
            def _():
                row_out(r).wait()
            return 0
        lax.fori_loop(0, tm, wait, 0)


def moe_experts(hn, idx, gates, w1, w3, w2, *, tm=512, tf=512):
    t, d = hn.shape
    ff = w1.shape[2]
    nf = ff // tf
    na = 2 * t
    nt = na // tm + N_EXPERTS
    rmax = nt * tm
    e_flat = idx.reshape(na)
    onehot = (e_flat[:, None] == jnp.arange(N_EXPERTS)[None, :]).astype(jnp.int32)
    csum = jnp.cumsum(onehot, axis=0)
    counts = csum[-1]
    rank = jnp.sum(csum * onehot, axis=1) - 1
    padded = ((counts + tm - 1) // tm) * tm
    ends = jnp.cumsum(padded)
    offs = ends - padded
    dest = offs[e_flat] + rank
    a_ids = jnp.arange(na, dtype=jnp.int32)
    row_tok = jnp.zeros((rmax,), jnp.int32).at[dest].set(a_ids % t)
    row_dst = jnp.full((rmax,), -1, jnp.int32).at[dest].set(a_ids)
    row_gate = jnp.zeros((rmax,), F32).at[dest].set(gates.reshape(na))
    tile_start = jnp.arange(nt, dtype=jnp.int32) * tm
    tile_valid = (tile_start < ends[-1]).astype(jnp.int32)
    tile_exp = jnp.minimum(jnp.sum((tile_start[:, None] >= ends[None, :]).astype(jnp.int32), axis=1),
                           N_EXPERTS - 1).astype(jnp.int32)
    last_e = tile_exp[jnp.maximum(ends[-1] // tm - 1, 0)]
    tile_exp = jnp.where(tile_valid > 0, tile_exp, last_e)
    gate_b = jnp.broadcast_to(row_gate[:, None], (rmax, LANES))

    def wmap(dim):
        def m(i, f, te, tv):
            fi = jnp.where(tv[i] > 0, f, nf - 1)
            return (te[i], 0, fi) if dim == 2 else (te[i], fi, 0)
        return m

    grid_spec = pltpu.PrefetchScalarGridSpec(
        num_scalar_prefetch=2,
        grid=(nt, nf),
        in_specs=[pl.BlockSpec((1, 1, tm), lambda i, f, te, tv: (i, 0, 0), memory_space=pltpu.SMEM),
                  pl.BlockSpec((1, 1, tm), lambda i, f, te, tv: (i, 0, 0), memory_space=pltpu.SMEM),
                  pl.BlockSpec((tm, LANES), lambda i, f, te, tv: (i, 0)),
                  pl.BlockSpec(memory_space=pl.ANY),
                  pl.BlockSpec((1, d, tf), wmap(2)),
                  pl.BlockSpec((1, d, tf), wmap(2)),
                  pl.BlockSpec((1, tf, d), wmap(1))],
        out_specs=pl.BlockSpec(memory_space=pl.ANY),
        scratch_shapes=[pltpu.VMEM((tm, d), F32),
                        pltpu.VMEM((tm, d), BF16),
                        pltpu.VMEM((tm, d), F32),
                        pltpu.VMEM((tm, d), F32),
                        pltpu.SemaphoreType.DMA(()),
                        pltpu.SemaphoreType.DMA(())],
    )
    return pl.pallas_call(
        functools.partial(_moe_kernel, tm=tm, nf=nf),
        grid_spec=grid_spec,
        out_shape=jax.ShapeDtypeStruct((na, d), F32),
        compiler_params=_cparams(("arbitrary",) * 2, VMEM_LIMIT_BYTES),
        name="moe_experts",
    )(tile_exp, tile_valid, row_tok.reshape(nt, 1, tm), row_dst.reshape(nt, 1, tm), gate_b,
      hn, w1, w3, w2)


def moe_layer(h, nw, router_w, w1, w3, w2):
    t, d = h.shape
    hn, idx, gates = moe_router(h, nw, router_w)
    return moe_experts(hn, idx, gates, w1, w3, w2).reshape(2, t, d)


def _pack_w_in(w):
    d = w.shape[0]
    n_ab = 2 * N_PAIRS_GDN * 2
    gdn_main = 4 * D_GDN
    rest = w[:, gdn_main + n_ab:]
    ab = jnp.pad(w[:, gdn_main:gdn_main + n_ab], ((0, 0), (0, LANES - n_ab)))
    return jnp.concatenate([w[:, :gdn_main], rest, ab], axis=1).astype(BF16)


def kernel(x, p, attn_norm_w, w_in, conv_w, a_log, dt_bias, gdn_norm_w, dsw_norm_w, sb_norm_w, w_out,
           ffn_norm_w, ffn_w1, ffn_w3, ffn_w2, router_w, moe_w1, moe_w3, moe_w2, ple_proj, ple_norm_w,
           ple_gate, final_norm_w):
    b, s, d = x.shape
    depth = w_in.shape[0]
    t = b * s
    h = x.reshape(t, d)
    n_dsw_heads = 2 * N_PAIRS_DSW
    slopes = jnp.asarray([2.0 ** (-8.0 * (i + 1) / n_dsw_heads) for i in range(n_dsw_heads)], F32)
    for i in range(depth):
        proj = norm_proj(h, attn_norm_w[i], _pack_w_in(w_in[i])).reshape(b, s, D_PROJ)
        o_gdn = gdn_mixer(proj, conv_w[i], a_log[i], dt_bias[i], gdn_norm_w[i]).reshape(t, D_GDN)
        outs, lses = [], []
        for _, r in DSW_CONFIGS:
            o_c, l_c = dsw_config(proj, slopes, r=r)
            outs.append(o_c)
            lses.append(l_c)
        o_dsw = dsw_combine(outs, lses, dsw_norm_w[i])
        o_sb = sb_attention(proj, sb_norm_w[i]).reshape(t, D_SB)
        h = mix_out(h, o_gdn, o_dsw, o_sb, w_out[i].astype(BF16))
        j = i // 2
        if i % 2 == 0:
            h = ffn_dense(h, ffn_norm_w[i], ffn_w1[j].astype(BF16), ffn_w3[j].astype(BF16),
                          ffn_w2[j].astype(BF16))
            extras = []
        else:
            y = moe_layer(h, ffn_norm_w[i], router_w[j], moe_w1[j].astype(BF16), moe_w3[j].astype(BF16),
                          moe_w2[j].astype(BF16))
            extras = [(y, 0), (y, 1)]
        h = ple_layer(h, extras, p[i].reshape(t, PLE_DIM), ple_proj[i].astype(BF16), ple_norm_w[i],
                      ple_gate[i].astype(BF16), final_norm_w, final=(i == depth - 1))
    return h.reshape(b, s, d)
```

```python
import functools
import math

import jax
import jax.numpy as jnp
import numpy as np
from jax import lax
from jax.experimental import pallas as pl
from jax.experimental.pallas import tpu as pltpu

F32 = jnp.float32
BF16 = jnp.bfloat16

D_MODEL = 1024
HEAD_DIM = 64
LANES = 128
N_PAIRS_GDN = 3
N_PAIRS_DSW = 3
N_PAIRS_SB = 2
D_GDN = 384
D_DSW = 384
D_SB = 256
CONV_WIDTH = 4
GDN_CHUNK = 64
DSW_CONFIGS = ((128, 1), (512, 4), (2048, 16))
DSW_SPAN = 128
D_FF = 2816
N_EXPERTS = 8
MOE_D_FF = 3584
PLE_DIM = 256
RMS_EPS = 1e-6
NEG_BIG = -1e30
SB_EXIT_LOG = -110.0

COL_GQ, COL_GK, COL_GV, COL_GZ = 0, 384, 768, 1152
COL_DQ, COL_DK, COL_DV = 1536, 1920, 2304
COL_SQ, COL_SK, COL_SV = 2688, 2944, 3200
COL_AB = 3456
D_PROJ = 3584

VMEM_LIMIT_BYTES = 56 * 1024 * 1024


def _cparams(sem, vmem=None):
    return pltpu.CompilerParams(dimension_semantics=sem, vmem_limit_bytes=vmem)


def _rms(x, w):
    ms = jnp.mean(x * x, axis=-1, keepdims=True)
    return x * lax.rsqrt(ms + RMS_EPS) * w


def _proj_kernel(h_ref, nw_ref, w_ref, o_ref, *, tn):
    hn = _rms(h_ref[...], nw_ref[...]).astype(BF16)
    n = w_ref.shape[1]
    for c in range(n // tn):
        o_ref[:, c * tn:(c + 1) * tn] = jnp.dot(
            hn, w_ref[:, c * tn:(c + 1) * tn], preferred_element_type=F32).astype(o_ref.dtype)


def norm_proj(h, nw, w, *, tm=512, tn=512):
    t, d = h.shape
    n = w.shape[1]
    return pl.pallas_call(
        functools.partial(_proj_kernel, tn=tn),
        grid=(t // tm,),
        in_specs=[pl.BlockSpec((tm, d), lambda i: (i, 0)),
                  pl.BlockSpec((1, d), lambda i: (0, 0)),
                  pl.BlockSpec((d, n), lambda i: (0, 0))],
        out_specs=pl.BlockSpec((tm, n), lambda i: (i, 0)),
        out_shape=jax.ShapeDtypeStruct((t, n), BF16),
        compiler_params=_cparams(("arbitrary",), VMEM_LIMIT_BYTES),
        name="norm_proj",
    )(h, nw.reshape(1, d), w)


def _mix_out_kernel(h_ref, a_ref, b_ref, c_ref, w_ref, o_ref):
    acc = jnp.dot(a_ref[...], w_ref[0:D_GDN, :], preferred_element_type=F32)
    acc += jnp.dot(b_ref[...], w_ref[D_GDN:D_GDN + D_DSW, :], preferred_element_type=F32)
    acc += jnp.dot(c_ref[...], w_ref[D_GDN + D_DSW:, :], preferred_element_type=F32)
    o_ref[...] = h_ref[...] + acc


def mix_out(h, o_gdn, o_dsw, o_sb, w_out, *, tm=512):
    t, d = h.shape
    return pl.pallas_call(
        _mix_out_kernel,
        grid=(t // tm,),
        in_specs=[pl.BlockSpec((tm, d), lambda i: (i, 0)),
                  pl.BlockSpec((tm, D_GDN), lambda i: (i, 0)),
                  pl.BlockSpec((tm, D_DSW), lambda i: (i, 0)),
                  pl.BlockSpec((tm, D_SB), lambda i: (i, 0)),
                  pl.BlockSpec((d, d), lambda i: (0, 0))],
        out_specs=pl.BlockSpec((tm, d), lambda i: (i, 0)),
        out_shape=jax.ShapeDtypeStruct((t, d), F32),
        compiler_params=_cparams(("arbitrary",), VMEM_LIMIT_BYTES),
        name="mix_out",
    )(h, o_gdn, o_dsw, o_sb, w_out)


def _ffn_kernel(h_ref, nw_ref, w1_ref, w3_ref, w2_ref, o_ref, *, tf):
    h = h_ref[...]
    hn = _rms(h, nw_ref[...]).astype(BF16)
    acc = h
    for c in range(w1_ref.shape[1] // tf):
        sl = slice(c * tf, (c + 1) * tf)
        a = jnp.dot(hn, w1_ref[:, sl], preferred_element_type=F32)
        b = jnp.dot(hn, w3_ref[:, sl], preferred_element_type=F32)
        g = (a * jax.nn.sigmoid(a) * b).astype(BF16)
        acc = acc + jnp.dot(g, w2_ref[sl, :], preferred_element_type=F32)
    o_ref[...] = acc


def ffn_dense(h, nw, w1, w3, w2, *, tm=512, tf=704):
    t, d = h.shape
    f = w1.shape[1]
    const = dict(pipeline_mode=pl.Buffered(1))
    return pl.pallas_call(
        functools.partial(_ffn_kernel, tf=tf),
        grid=(t // tm,),
        in_specs=[pl.BlockSpec((tm, d), lambda i: (i, 0)),
                  pl.BlockSpec((1, d), lambda i: (0, 0)),
                  pl.BlockSpec((d, f), lambda i: (0, 0), **const),
                  pl.BlockSpec((d, f), lambda i: (0, 0), **const),
                  pl.BlockSpec((f, d), lambda i: (0, 0), **const)],
        out_specs=pl.BlockSpec((tm, d), lambda i: (i, 0)),
        out_shape=jax.ShapeDtypeStruct((t, d), F32),
        compiler_params=_cparams(("arbitrary",), VMEM_LIMIT_BYTES),
        name="ffn_dense",
    )(h, nw.reshape(1, d), w1, w3, w2)


def _ple_kernel(*refs, moe, final):
    h_ref = refs[0]
    p_ref, proj_ref, nw_ref, gate_ref, fw_ref, o_ref = refs[-6:]
    h = h_ref[...]
    if moe:
        y0_ref, y1_ref, g_ref = refs[1:4]
        g = g_ref[...]
        h = h + g[:, 0:1] * y0_ref[...] + g[:, HEAD_DIM:HEAD_DIM + 1] * y1_ref[...]
    hn = _rms(h, nw_ref[...]).astype(BF16)
    gate = jax.nn.sigmoid(jnp.dot(hn, gate_ref[...], preferred_element_type=F32))
    emb = jnp.dot(p_ref[...].astype(BF16), proj_ref[...], preferred_element_type=F32)
    out = h + emb * gate
    if final:
        out = _rms(out, fw_ref[...])
    o_ref[...] = out


def ple_layer(h, moe_out, p, proj, nw, gate, final_w, *, final, tm=512):
    t, d = h.shape
    in_specs = [pl.BlockSpec((tm, d), lambda i: (i, 0))]
    args = [h]
    if moe_out is not None:
        y, gates = moe_out
        in_specs += [pl.BlockSpec((tm, d), lambda i: (i, 0)),
                     pl.BlockSpec((tm, d), lambda i: (t // tm + i, 0)),
                     pl.BlockSpec((tm, LANES), lambda i: (i, 0))]
        args += [y, y, gates]
    in_specs += [pl.BlockSpec((tm, PLE_DIM), lambda i: (i, 0)),
                 pl.BlockSpec((PLE_DIM, d), lambda i: (0, 0)),
                 pl.BlockSpec((1, d), lambda i: (0, 0)),
                 pl.BlockSpec((d, d), lambda i: (0, 0)),
                 pl.BlockSpec((1, d), lambda i: (0, 0))]
    args += [p, proj, nw.reshape(1, d), gate, final_w.reshape(1, d)]
    return pl.pallas_call(
        functools.partial(_ple_kernel, moe=moe_out is not None, final=final),
        grid=(t // tm,),
        in_specs=in_specs,
        out_specs=pl.BlockSpec((tm, d), lambda i: (i, 0)),
        out_shape=jax.ShapeDtypeStruct((t, d), F32),
        compiler_params=_cparams(("arbitrary",), VMEM_LIMIT_BYTES),
        name="ple_layer",
    )(*args)


def _lane_iota(shape):
    return lax.broadcasted_iota(jnp.int32, shape, len(shape) - 1)


def _head_sumsq(x):
    lo = _lane_iota(x.shape) < HEAD_DIM
    sq = x * x
    s_lo = jnp.sum(jnp.where(lo, sq, 0.0), axis=-1, keepdims=True)
    s_hi = jnp.sum(jnp.where(lo, 0.0, sq), axis=-1, keepdims=True)
    return jnp.where(lo, s_lo, s_hi)


def _head_rms(x, w2):
    return x * lax.rsqrt(_head_sumsq(x) * (1.0 / HEAD_DIM) + RMS_EPS) * w2


def _stack_heads(x):
    lo = _lane_iota(x.shape) < HEAD_DIM
    zero = jnp.zeros_like(x)
    return jnp.concatenate([jnp.where(lo, x, zero), jnp.where(lo, zero, x)], axis=0)


def _round_robin(gens):
    gens = list(gens)
    while gens:
        alive = []
        for g in gens:
            try:
                next(g)
                alive.append(g)
            except StopIteration:
                pass
        gens = alive


def _dot(a, b):
    return jnp.dot(a, b, preferred_element_type=F32)


def _dot_nt(a, b):
    return lax.dot_general(a, b, (((1,), (1,)), ((), ())), preferred_element_type=F32)


def _dot_tn(a, b):
    return lax.dot_general(a, b, (((0,), (0,)), ((), ())), preferred_element_type=F32)


def _sb_kernel(q_ref, k_ref, v_ref, nw_ref, o_ref, *, tq):
    i = pl.program_id(2)
    lane = _lane_iota((1, LANES))
    row = lax.broadcasted_iota(jnp.int32, (tq, tq), 0)
    col = lax.broadcasted_iota(jnp.int32, (tq, tq), 1)
    upper = jnp.where(row > col, 1.0, 0.0).astype(BF16)
    diag_before = col < row
    q = q_ref[0] * jnp.asarray(HEAD_DIM ** -0.5, BF16)
    zero = jnp.zeros_like(q)
    qh = (jnp.where(lane < HEAD_DIM, q, zero), jnp.where(lane < HEAD_DIM, zero, q))

    def body(carry):
        jj, _, heads = carry
        j = i - jj
        start = pl.multiple_of(j * tq, tq)
        kj = k_ref[0, pl.ds(start, tq), :]
        vj = v_ref[0, pl.ds(start, tq), :]
        before = jnp.logical_or(diag_before, jj > 0)
        out = []
        for hh in range(2):
            acc, c = heads[hh]
            z = _dot_nt(qh[hh], kj)
            sp = jnp.maximum(z, 0.0) + jnp.log(1.0 + jnp.exp(-jnp.abs(z)))
            lk = jnp.where(before, -sp, 0.0)
            lk_hi = lk.astype(BF16)
            lk_lo = (lk - lk_hi.astype(F32)).astype(BF16)
            suffix = _dot(lk_hi, upper) + _dot(lk_lo, upper)
            a = jnp.where(before, jnp.exp(z - sp + suffix + c), 0.0)
            acc = acc + _dot(a.astype(BF16), vj)
            c = c + jnp.sum(lk, axis=-1, keepdims=True)
            out.append((acc, c))
        cmax = jnp.maximum(jnp.max(out[0][1]), jnp.max(out[1][1]))
        return jj + 1, cmax, tuple(out)

    def cond(carry):
        jj, cmax, _ = carry
        return jnp.logical_and(jj <= i, cmax > SB_EXIT_LOG)

    init = tuple((jnp.zeros((tq, LANES), F32), jnp.zeros((tq, 1), F32)) for _ in range(2))
    _, _, ((acc0, _), (acc1, _)) = lax.while_loop(cond, body, (jnp.int32(0), jnp.float32(0.0), init))
    o = jnp.where(lane < HEAD_DIM, acc0, acc1)
    o_ref[0] = _head_rms(o, nw_ref[...]).astype(o_ref.dtype)


def sb_attention(proj, norm_w, *, tq=256):
    b, s, _ = proj.shape
    cq, ck, cv = COL_SQ // LANES, COL_SK // LANES, COL_SV // LANES
    nw2 = jnp.tile(norm_w.astype(F32), 2).reshape(1, LANES)
    return pl.pallas_call(
        functools.partial(_sb_kernel, tq=tq),
        grid=(b, N_PAIRS_SB, s // tq),
        in_specs=[pl.BlockSpec((1, tq, LANES), lambda bi, p, i: (bi, i, cq + p)),
                  pl.BlockSpec((1, s, LANES), lambda bi, p, i: (bi, 0, ck + p)),
                  pl.BlockSpec((1, s, LANES), lambda bi, p, i: (bi, 0, cv + p)),
                  pl.BlockSpec((1, LANES), lambda bi, p, i: (0, 0))],
        out_specs=pl.BlockSpec((1, tq, LANES), lambda bi, p, i: (bi, i, p)),
        out_shape=jax.ShapeDtypeStruct((b, s, D_SB), BF16),
        compiler_params=_cparams(("arbitrary",) * 3, VMEM_LIMIT_BYTES),
        name="sb_attention",
    )(proj, proj, proj, nw2)


DSW_PERM_ROWS = 1024


def _dsw_kernel(slope_ref, q_ref, k_ref, v_ref, dist_ref, nw_ref, o_ref,
                qf, kf, vf, qp, kp, vp, op, lp, oacc, lacc, *, tq):
    hp = pl.program_id(1)
    seq = q_ref.shape[1]
    lane = _lane_iota((1, LANES))
    lo = lane < HEAD_DIM
    nperm = seq // DSW_PERM_ROWS

    def to_f32(m, _):
        rows = pl.ds(pl.multiple_of(m * DSW_PERM_ROWS, DSW_PERM_ROWS), DSW_PERM_ROWS)
        for src, dst in ((q_ref, qf), (k_ref, kf), (v_ref, vf)):
            dst[rows, :] = src[0, rows, :].astype(F32)
        return 0

    lax.fori_loop(0, nperm, to_f32, 0)

    def attend(qs, ks, vs, o_dst, l_dst, r):
        length = seq // r
        tb = min(tq, length)
        nblk = length // tb
        slopes = [slope_ref[2 * hp + hh] * float(r) for hh in range(2)]

        def body(it, _):
            n = it % nblk
            start = pl.multiple_of(it * tb, tb)
            pstart = pl.multiple_of(jnp.maximum(start - DSW_SPAN, 0), DSW_SPAN)
            q = qs[pl.ds(start, tb), :] * jnp.asarray(HEAD_DIM ** -0.5, BF16)
            kc = jnp.concatenate([ks[pl.ds(pstart, DSW_SPAN), :], ks[pl.ds(start, tb), :]], axis=0)
            vc = jnp.concatenate([vs[pl.ds(pstart, DSW_SPAN), :], vs[pl.ds(start, tb), :]], axis=0)
            dist = dist_ref[jnp.minimum(n, 1)]
            zero = jnp.zeros_like(q)
            os, ls = [], []
            for hh in range(2):
                qh = jnp.where(lo, q, zero) if hh == 0 else jnp.where(lo, zero, q)
                sc = _dot_nt(qh, kc) - slopes[hh] * dist
                m = jnp.max(sc, axis=-1, keepdims=True)
                e = jnp.exp(sc - m)
                den = jnp.sum(e, axis=-1, keepdims=True)
                os.append(_dot(e.astype(BF16), vc) / den)
                ls.append(m + jnp.log(den))
            o_dst[pl.ds(start, tb), :] = jnp.where(lo, os[0], os[1])
            l_dst[pl.ds(start, tb), :] = jnp.where(lo, ls[0], ls[1])
            return 0

        lax.fori_loop(0, r * nblk, body, 0)

    for _, r in DSW_CONFIGS:
        if r == 1:
            attend(q_ref.at[0], k_ref.at[0], v_ref.at[0], oacc, lacc, 1)
            continue
        length = seq // r
        per = DSW_PERM_ROWS // r

        def permute(m, _, r=r, length=length, per=per):
            base = pl.multiple_of(m * DSW_PERM_ROWS, DSW_PERM_ROWS)
            for c in range(r):
                dst = pl.ds(pl.multiple_of(c * length + m * per, per), per)
                for src, dstref in ((qf, qp), (kf, kp), (vf, vp)):
                    dstref[dst, :] = src.at[pl.ds(base, DSW_PERM_ROWS)][pl.ds(c, per, stride=r), :].astype(BF16)
            return 0

        lax.fori_loop(0, nperm, permute, 0)
        attend(qp, kp, vp, op, lp, r)

        def merge(m, _, r=r, length=length, per=per):
            base = pl.multiple_of(m * DSW_PERM_ROWS, DSW_PERM_ROWS)
            for c in range(r):
                src = pl.ds(pl.multiple_of(c * length + m * per, per), per)
                rows = pl.ds(c, per, stride=r)
                o_view = oacc.at[pl.ds(base, DSW_PERM_ROWS)]
                l_view = lacc.at[pl.ds(base, DSW_PERM_ROWS)]
                o_old, l_old = o_view[rows, :], l_view[rows, :]
                o_new, l_new = op[src, :], lp[src, :]
                l_max = jnp.maximum(l_old, l_new)
                w_old, w_new = jnp.exp(l_old - l_max), jnp.exp(l_new - l_max)
                tot = w_old + w_new
                o_view[rows, :] = (w_old * o_old + w_new * o_new) / tot
                l_view[rows, :] = l_max + jnp.log(tot)
            return 0

        lax.fori_loop(0, nperm, merge, 0)

    def finish(m, _):
        rows = pl.ds(pl.multiple_of(m * DSW_PERM_ROWS, DSW_PERM_ROWS), DSW_PERM_ROWS)
        o_ref[0, rows, :] = _head_rms(oacc[rows, :], nw_ref[...]).astype(o_ref.dtype)
        return 0

    lax.fori_loop(0, nperm, finish, 0)


def _dsw_dist_table(tq):
    qi = np.arange(tq)[:, None]
    ki = np.arange(tq + DSW_SPAN)[None, :]
    dist = qi - ki + DSW_SPAN
    valid = (dist >= 0) & (dist <= DSW_SPAN)
    later = np.where(valid, dist, 1e30).astype(np.float32)
    first = np.where(valid & (ki >= DSW_SPAN), dist, 1e30).astype(np.float32)
    return jnp.asarray(np.stack([first, later]))


def dsw_mixer(proj, slopes, norm_w, *, tq=256):
    b, s, _ = proj.shape
    cq, ck, cv = COL_DQ // LANES, COL_DK // LANES, COL_DV // LANES
    nw2 = jnp.tile(norm_w.astype(F32), 2).reshape(1, LANES)
    dist = _dsw_dist_table(tq)

    def col(base):
        return pl.BlockSpec((1, s, LANES), lambda bi, p: (bi, 0, base + p))

    return pl.pallas_call(
        functools.partial(_dsw_kernel, tq=tq),
        grid=(b, N_PAIRS_DSW),
        in_specs=[pl.BlockSpec(memory_space=pltpu.SMEM),
                  col(cq), col(ck), col(cv),
                  pl.BlockSpec((2, tq, tq + DSW_SPAN), lambda bi, p: (0, 0, 0)),
                  pl.BlockSpec((1, LANES), lambda bi, p: (0, 0))],
        out_specs=pl.BlockSpec((1, s, LANES), lambda bi, p: (bi, 0, p)),
        out_shape=jax.ShapeDtypeStruct((b, s, D_DSW), BF16),
        scratch_shapes=[pltpu.VMEM((s, LANES), F32)] * 3
        + [pltpu.VMEM((s, LANES), BF16)] * 3
        + [pltpu.VMEM((s, LANES), F32)] * 4,
        compiler_params=_cparams(("arbitrary",) * 2, VMEM_LIMIT_BYTES),
        name="dsw_mixer",
    )(slopes, proj, proj, proj, dist, nw2)


def _gdn_kernel(xq_ref, xk_ref, xv_ref, z_ref, ab_ref, cw_ref, expand_ref, alog_ref, dtb_ref, nw_ref, o_ref,
                u_s, w_s, qd_s, kd_s, qk_s, gl_s, *, group):
    c = GDN_CHUNK
    seq = xq_ref.shape[1]
    nchunk = seq // c
    lane = _lane_iota((c, LANES))
    rowi = lax.broadcasted_iota(jnp.int32, (c, LANES), 0)
    colj = jnp.where(lane < HEAD_DIM, lane, lane - HEAD_DIM)
    causal = rowi >= colj
    strict = rowi > colj
    eye_cat = jnp.where(rowi == colj, 1.0, 0.0)
    r2 = lax.broadcasted_iota(jnp.int32, (LANES, LANES), 0)
    c2 = lax.broadcasted_iota(jnp.int32, (LANES, LANES), 1)
    same_head = (r2 < HEAD_DIM) == (c2 < HEAD_DIM)
    nw2 = nw_ref[...]

    def conv_silu(x_ref, which, p, start, has_prev):
        lanes = slice(p * LANES, (p + 1) * LANES)
        w = cw_ref[:, which * D_GDN + p * LANES:which * D_GDN + (p + 1) * LANES]
        x = x_ref[0, pl.ds(start, c), lanes].astype(F32)
        pstart = pl.multiple_of(jnp.maximum(start - 16, 0), 16)
        prev = x_ref[0, pl.ds(pstart, 16), lanes].astype(F32)[8:]
        xx = jnp.concatenate([jnp.where(has_prev, prev, 0.0), x], axis=0)
        y = xx[8:] * w[3:4]
        for s in (1, 2, 3):
            y = y + pltpu.roll(xx, s, axis=0)[8:] * w[3 - s:4 - s]
        return y * jax.nn.sigmoid(y)

    def chunk_local(p, n):
        start = pl.multiple_of(n * c, c)
        has_prev = n > 0
        q = conv_silu(xq_ref, 0, p, start, has_prev)
        k = conv_silu(xk_ref, 1, p, start, has_prev)
        v = conv_silu(xv_ref, 2, p, start, has_prev)
        q = q * lax.rsqrt(_head_sumsq(q) + RMS_EPS) * (HEAD_DIM ** -0.5)
        k = k * lax.rsqrt(_head_sumsq(k) + RMS_EPS)
        ab = _dot(ab_ref[0, pl.ds(start, c), :], expand_ref[p])
        a_e, b_e = ab[:, :LANES], ab[:, LANES:]
        xg = a_e + dtb_ref[p]
        g = -jnp.exp(alog_ref[p]) * (jnp.maximum(xg, 0.0) + jnp.log(1.0 + jnp.exp(-jnp.abs(xg))))
        beta = jax.nn.sigmoid(b_e)
        gcol = g
        for s in (1, 2, 4, 8, 16, 32):
            gcol = gcol + jnp.where(rowi >= s, pltpu.roll(gcol, s, axis=0), 0.0)
        grow = jnp.sum(jnp.where(rowi == colj, gcol, 0.0), axis=0, keepdims=True)
        glast = gcol[c - 1:c, :]
        diff = gcol - grow
        decay = jnp.exp(jnp.where(causal, diff, NEG_BIG))
        eg = jnp.exp(gcol)
        kb = k.astype(BF16)
        ks = _stack_heads(kb)
        lhs = jnp.concatenate([kb, q.astype(BF16)], axis=0)
        rhs = jnp.concatenate([_stack_heads((v * beta).astype(BF16)),
                               _stack_heads((k * beta * eg).astype(BF16))], axis=1)
        yield
        kq = _dot_nt(lhs, ks)
        kk_cat, qk_cat = kq[:c], kq[c:]
        pm = jnp.where(strict, -(beta * kk_cat * decay), 0.0)
        t = eye_cat + pm
        pb = pm.astype(BF16)
        pbs = _stack_heads(pb)
        yield
        pm = _dot(pb, pbs)
        for _ in range(4):
            pb = pm.astype(BF16)
            lhs = jnp.concatenate([t.astype(BF16), pb], axis=0)
            pbs = _stack_heads(pb)
            yield
            res = _dot(lhs, pbs)
            t = t + res[:c]
            pm = res[c:]
        pb = pm.astype(BF16)
        tb = t.astype(BF16)
        pbs = _stack_heads(pb)
        yield
        t = t + _dot(tb, pbs)
        tb = t.astype(BF16)
        yield
        uw = _dot(tb, rhs)
        u, w = uw[:, :LANES], uw[:, LANES:]
        rows = pl.ds(start, c)
        u_s[p, rows, :] = u
        w_s[p, rows, :] = w.astype(BF16)
        qd_s[p, rows, :] = (q * eg).astype(BF16)
        kd_s[p, rows, :] = (k * jnp.exp(glast - gcol)).astype(BF16)
        qk_s[p, rows, :] = jnp.where(causal, qk_cat * decay, 0.0).astype(BF16)
        gl_s[p, n] = jnp.broadcast_to(jnp.exp(glast), (8, LANES))

    for p in range(N_PAIRS_GDN):
        def local_body(m, _, p=p):
            _round_robin([chunk_local(p, m * group + gi) for gi in range(group)])
            return 0
        lax.fori_loop(0, nchunk // group, local_body, 0)

    def state_step(p, n, state, result):
        start = pl.multiple_of(n * c, c)
        rows = pl.ds(start, c)
        lanes = slice(p * LANES, (p + 1) * LANES)
        lhs = jnp.concatenate([w_s[p, rows, :], qd_s[p, rows, :]], axis=0)
        sb = state.astype(BF16)
        yield
        ws = _dot(lhs, sb)
        v_new = u_s[p, rows, :] - ws[:c]
        vb = v_new.astype(BF16)
        vbs = _stack_heads(vb)
        yield
        o = ws[c:] + _dot(qk_s[p, rows, :], vbs)
        upd = _dot_tn(kd_s[p, rows, :], vb)
        result[p] = state * gl_s[p, n][0:1] + jnp.where(same_head, upd, 0.0)
        zg = z_ref[0, rows, lanes].astype(F32)
        out = _head_rms(o, nw2) * (zg * jax.nn.sigmoid(zg))
        o_ref[0, rows, lanes] = out.astype(o_ref.dtype)

    def state_body(n, states):
        result = [None] * N_PAIRS_GDN
        _round_robin([state_step(p, n, states[p], result) for p in range(N_PAIRS_GDN)])
        return tuple(result)

    lax.fori_loop(0, nchunk, state_body, tuple(jnp.zeros((LANES, LANES), F32) for _ in range(N_PAIRS_GDN)))


def gdn_mixer(proj, conv_w, a_log, dt_bias, norm_w):
    b, s, _ = proj.shape
    npair = N_PAIRS_GDN
    ex = np.zeros((npair, LANES, 2 * LANES), np.float32)
    for p in range(npair):
        for hh in range(2):
            ex[p, 2 * p + hh, hh * HEAD_DIM:(hh + 1) * HEAD_DIM] = 1.0
            ex[p, 2 * npair + 2 * p + hh, LANES + hh * HEAD_DIM:LANES + (hh + 1) * HEAD_DIM] = 1.0
    expand = jnp.asarray(ex, BF16)
    alog2 = jnp.repeat(a_log.astype(F32).reshape(npair, 1, 2), HEAD_DIM, axis=2)
    dtb2 = jnp.repeat(dt_bias.astype(F32).reshape(npair, 1, 2), HEAD_DIM, axis=2)
    nw2 = jnp.tile(norm_w.astype(F32), 2).reshape(1, LANES)
    cw = conv_w.astype(F32)
    once = dict(pipeline_mode=pl.Buffered(1))

    def xcol(base):
        return pl.BlockSpec((1, s, D_GDN), lambda bi: (bi, 0, base // D_GDN), **once)

    def whole(shape):
        return pl.BlockSpec(shape, lambda bi: (0,) * len(shape))

    return pl.pallas_call(
        functools.partial(_gdn_kernel, group=4),
        grid=(b,),
        in_specs=[xcol(COL_GQ), xcol(COL_GK), xcol(COL_GV), xcol(COL_GZ),
                  pl.BlockSpec((1, s, LANES), lambda bi: (bi, 0, COL_AB // LANES), **once),
                  whole((CONV_WIDTH, 3 * D_GDN)),
                  whole((npair, LANES, 2 * LANES)),
                  whole((npair, 1, LANES)),
                  whole((npair, 1, LANES)),
                  whole((1, LANES))],
        out_specs=pl.BlockSpec((1, s, D_GDN), lambda bi: (bi, 0, 0)),
        out_shape=jax.ShapeDtypeStruct((b, s, D_GDN), BF16),
        scratch_shapes=[pltpu.VMEM((npair, s, LANES), F32)]
        + [pltpu.VMEM((npair, s, LANES), BF16)] * 4
        + [pltpu.VMEM((npair, s // GDN_CHUNK, 8, LANES), F32)],
        compiler_params=_cparams(("arbitrary",), VMEM_LIMIT_BYTES),
        name="gdn_mixer",
    )(proj, proj, proj, proj, proj, cw, expand, alog2, dtb2, nw2)


def _router_kernel(h_ref, nw_ref, rw_ref, hn_ref, idx_ref, gate_ref):
    hn = _rms(h_ref[...], nw_ref[...])
    hn_ref[...] = hn
    lt = lax.dot_general(rw_ref[...], hn, (((1,), (1,)), ((), ())),
                         preferred_element_type=F32, precision=lax.Precision.HIGHEST)
    row = lax.broadcasted_iota(jnp.int32, lt.shape, 0)
    v1 = jnp.max(lt, axis=0, keepdims=True)
    i1 = jnp.min(jnp.where(lt == v1, row, N_EXPERTS), axis=0, keepdims=True)
    lt2 = jnp.where(row == i1, -jnp.inf, lt)
    v2 = jnp.max(lt2, axis=0, keepdims=True)
    i2 = jnp.min(jnp.where(lt2 == v2, row, N_EXPERTS), axis=0, keepdims=True)
    e = jnp.exp(v2 - v1)
    g1 = 1.0 / (1.0 + e)
    idx_ref[...] = jnp.concatenate([i1, i2], axis=0)
    g8 = jnp.concatenate([g1, e * g1, jnp.zeros((N_EXPERTS - 2, g1.shape[1]), F32)], axis=0)
    srow = lax.broadcasted_iota(jnp.int32, (N_EXPERTS, LANES), 0)
    slane = lax.broadcasted_iota(jnp.int32, (N_EXPERTS, LANES), 1)
    sel = jnp.where(srow == slane // HEAD_DIM, 1.0, 0.0)
    gate_ref[...] = lax.dot_general(g8, sel, (((0,), (0,)), ((), ())),
                                    preferred_element_type=F32, precision=lax.Precision.HIGHEST)


def moe_router(h, nw, router_w, *, tm=512):
    t, d = h.shape
    return pl.pallas_call(
        _router_kernel,
        grid=(t // tm,),
        in_specs=[pl.BlockSpec((tm, d), lambda i: (i, 0)),
                  pl.BlockSpec((1, d), lambda i: (0, 0)),
                  pl.BlockSpec((N_EXPERTS, d), lambda i: (0, 0))],
        out_specs=[pl.BlockSpec((tm, d), lambda i: (i, 0)),
                   pl.BlockSpec((2, tm), lambda i: (0, i)),
                   pl.BlockSpec((tm, LANES), lambda i: (i, 0))],
        out_shape=[jax.ShapeDtypeStruct((t, d), F32),
                   jax.ShapeDtypeStruct((2, t), jnp.int32),
                   jax.ShapeDtypeStruct((t, LANES), F32)],
        compiler_params=_cparams(("arbitrary",), VMEM_LIMIT_BYTES),
        name="moe_router",
    )(h, nw.reshape(1, d), router_w.astype(F32).T)


def _moe_kernel(te_ref, tv_ref, tok_ref, tokn_ref, dst_ref, hn_hbm, w1_ref, w3_ref, w2_ref, y_hbm,
                xbuf, xb, acc, ybuf, gsem, ssem, pending, *, tm, nf):
    i = pl.program_id(0)
    f = pl.program_id(1)
    nt = pl.num_programs(0)
    valid = tv_ref[i] > 0
    nxt = jnp.minimum(i + 1, nt - 1)
    valid_next = jnp.logical_and(i + 1 < nt, tv_ref[nxt] > 0)

    def gather_start(tref):
        def body(r, _):
            pltpu.make_async_copy(hn_hbm.at[pl.ds(tref[0, 0, r], 1)], xbuf.at[pl.ds(r, 1)], gsem).start()
            return 0
        lax.fori_loop(0, tm, body, 0, unroll=8)

    def scatter_start():
        def body(r, _):
            pltpu.make_async_copy(ybuf.at[pl.ds(r, 1)], y_hbm.at[pl.ds(dst_ref[0, 0, r], 1)], ssem).start()
            return 0
        lax.fori_loop(0, tm, body, 0, unroll=8)

    def gather_wait():
        pltpu.make_async_copy(xbuf, xbuf, gsem).wait()

    def scatter_wait():
        pltpu.make_async_copy(ybuf, ybuf, ssem).wait()

    @pl.when(jnp.logical_and(i == 0, f == 0))
    def _():
        pending[0] = 0
        gather_start(tok_ref)
        ybuf[...] = jnp.zeros_like(ybuf)
        tail = pltpu.make_async_copy(ybuf, y_hbm.at[pl.ds(y_hbm.shape[0] - tm, tm)], ssem)
        tail.start()
        tail.wait()

    @pl.when(jnp.logical_and(valid, f == 0))
    def _():
        gather_wait()
        xb[...] = xbuf[...].astype(BF16)
        acc[...] = jnp.zeros_like(acc)

    @pl.when(jnp.logical_and(valid_next, f == 0))
    def _():
        gather_start(tokn_ref)

    @pl.when(valid)
    def _():
        x = xb[...]
        a = _dot(x, w1_ref[0])
        b = _dot(x, w3_ref[0])
        g = (a * jax.nn.sigmoid(a) * b).astype(BF16)
        acc[...] += _dot(g, w2_ref[0])

    @pl.when(jnp.logical_and(valid, f == nf - 1))
    def _():
        @pl.when(pending[0] == 1)
        def _():
            scatter_wait()
        ybuf[...] = acc[...]
        scatter_start()
        pending[0] = 1

    @pl.when(jnp.logical_and(i == nt - 1, f == nf - 1))
    def _():
        @pl.when(pending[0] == 1)
        def _():
            scatter_wait()
        pending[0] = 0


def moe_experts(hn, idx, w1, w3, w2, *, tm=512, tf=512):
    t, d = hn.shape
    ff = w1.shape[2]
    nf = ff // tf
    na = 2 * t
    nt = na // tm + N_EXPERTS
    rmax = nt * tm
    e_flat = idx.reshape(na)
    onehot = (e_flat[:, None] == jnp.arange(N_EXPERTS)[None, :]).astype(jnp.int32)
    csum = jnp.cumsum(onehot, axis=0)
    counts = csum[-1]
    rank = jnp.sum(csum * onehot, axis=1) - 1
    padded = ((counts + tm - 1) // tm) * tm
    ends = jnp.cumsum(padded)
    offs = ends - padded
    dest = offs[e_flat] + rank
    a_ids = jnp.arange(na, dtype=jnp.int32)
    row_asg = jnp.full((rmax,), -1, jnp.int32).at[dest].set(a_ids, unique_indices=True)
    row_real = row_asg >= 0
    row_tok = jnp.where(row_real, row_asg % t, 0)
    row_dst = jnp.where(row_real, row_asg, na + jnp.arange(rmax, dtype=jnp.int32) % tm)
    tile_start = jnp.arange(nt, dtype=jnp.int32) * tm
    tile_valid = (tile_start < ends[-1]).astype(jnp.int32)
    tile_exp = jnp.minimum(jnp.sum((tile_start[:, None] >= ends[None, :]).astype(jnp.int32), axis=1),
                           N_EXPERTS - 1).astype(jnp.int32)
    last_e = tile_exp[jnp.maximum(ends[-1] // tm - 1, 0)]
    tile_exp = jnp.where(tile_valid > 0, tile_exp, last_e)
    row_tok = row_tok.reshape(nt, 1, tm)

    def wmap(dim):
        def m(i, f, te, tv):
            fi = jnp.where(tv[i] > 0, f, nf - 1)
            return (te[i], 0, fi) if dim == 2 else (te[i], fi, 0)
        return m

    grid_spec = pltpu.PrefetchScalarGridSpec(
        num_scalar_prefetch=2,
        grid=(nt, nf),
        in_specs=[pl.BlockSpec((1, 1, tm), lambda i, f, te, tv: (i, 0, 0), memory_space=pltpu.SMEM),
                  pl.BlockSpec((1, 1, tm), lambda i, f, te, tv: (jnp.minimum(i + 1, nt - 1), 0, 0),
                               memory_space=pltpu.SMEM),
                  pl.BlockSpec((1, 1, tm), lambda i, f, te, tv: (i, 0, 0), memory_space=pltpu.SMEM),
                  pl.BlockSpec(memory_space=pl.ANY),
                  pl.BlockSpec((1, d, tf), wmap(2)),
                  pl.BlockSpec((1, d, tf), wmap(2)),
                  pl.BlockSpec((1, tf, d), wmap(1))],
        out_specs=pl.BlockSpec(memory_space=pl.ANY),
        scratch_shapes=[pltpu.VMEM((tm, d), F32),
                        pltpu.VMEM((tm, d), BF16),
                        pltpu.VMEM((tm, d), F32),
                        pltpu.VMEM((tm, d), F32),
                        pltpu.SemaphoreType.DMA(()),
                        pltpu.SemaphoreType.DMA(()),
                        pltpu.SMEM((1,), jnp.int32)],
    )
    return pl.pallas_call(
        functools.partial(_moe_kernel, tm=tm, nf=nf),
        grid_spec=grid_spec,
        out_shape=jax.ShapeDtypeStruct((na + tm, d), F32),
        compiler_params=_cparams(("arbitrary",) * 2, VMEM_LIMIT_BYTES),
        name="moe_experts",
    )(tile_exp, tile_valid, row_tok, row_tok, row_dst.reshape(nt, 1, tm), hn, w1, w3, w2)


def moe_layer(h, nw, router_w, w1, w3, w2):
    hn, idx, gates = moe_router(h, nw, router_w)
    return moe_experts(hn, idx, w1, w3, w2), gates


def _pack_w_in(w):
    d = w.shape[0]
    n_ab = 2 * N_PAIRS_GDN * 2
    gdn_main = 4 * D_GDN
    rest = w[:, gdn_main + n_ab:]
    ab = jnp.pad(w[:, gdn_main:gdn_main + n_ab], ((0, 0), (0, LANES - n_ab)))
    return jnp.concatenate([w[:, :gdn_main], rest, ab], axis=1).astype(BF16)


def kernel(x, p, attn_norm_w, w_in, conv_w, a_log, dt_bias, gdn_norm_w, dsw_norm_w, sb_norm_w, w_out,
           ffn_norm_w, ffn_w1, ffn_w3, ffn_w2, router_w, moe_w1, moe_w3, moe_w2, ple_proj, ple_norm_w,
           ple_gate, final_norm_w):
    b, s, d = x.shape
    depth = w_in.shape[0]
    t = b * s
    h = x.reshape(t, d)
    n_dsw_heads = 2 * N_PAIRS_DSW
    slopes = jnp.asarray([2.0 ** (-8.0 * (i + 1) / n_dsw_heads) for i in range(n_dsw_heads)], F32)
    for i in range(depth):
        proj = norm_proj(h, attn_norm_w[i], _pack_w_in(w_in[i])).reshape(b, s, D_PROJ)
        o_gdn = gdn_mixer(proj, conv_w[i], a_log[i], dt_bias[i], gdn_norm_w[i]).reshape(t, D_GDN)
        o_dsw = dsw_mixer(proj, slopes, dsw_norm_w[i]).reshape(t, D_DSW)
        o_sb = sb_attention(proj, sb_norm_w[i]).reshape(t, D_SB)
        h = mix_out(h, o_gdn, o_dsw, o_sb, w_out[i].astype(BF16))
        j = i // 2
        if i % 2 == 0:
            h = ffn_dense(h, ffn_norm_w[i], ffn_w1[j].astype(BF16), ffn_w3[j].astype(BF16),
                          ffn_w2[j].astype(BF16))
            moe_out = None
        else:
            moe_out = moe_layer(h, ffn_norm_w[i], router_w[j], moe_w1[j].astype(BF16), moe_w3[j].astype(BF16),
                                moe_w2[j].astype(BF16))
        h = ple_layer(h, moe_out, p[i].reshape(t, PLE_DIM), ple_proj[i].astype(BF16), ple_norm_w[i],
                      ple_gate[i].astype(BF16), final_norm_w, final=(i == depth - 1))
    return h.reshape(b, s, d)
```

```python
import functools
import math

import jax
import jax.numpy as jnp
import numpy as np
from jax import lax
from jax.experimental import pallas as pl
from jax.experimental.pallas import tpu as pltpu

F32 = jnp.float32
BF16 = jnp.bfloat16

D_MODEL = 1024
HEAD_DIM = 64
LANES = 128
N_PAIRS_GDN = 3
N_PAIRS_DSW = 3
N_PAIRS_SB = 2
D_GDN = 384
D_DSW = 384
D_SB = 256
CONV_WIDTH = 4
GDN_CHUNK = 64
DSW_CONFIGS = ((128, 1), (512, 4), (2048, 16))
DSW_SPAN = 128
D_FF = 2816
N_EXPERTS = 8
MOE_D_FF = 3584
PLE_DIM = 256
RMS_EPS = 1e-6
NEG_BIG = -1e30
SB_EXIT_LOG = -110.0

COL_GQ, COL_GK, COL_GV, COL_GZ = 0, 384, 768, 1152
COL_DQ, COL_DK, COL_DV = 1536, 1920, 2304
COL_SQ, COL_SK, COL_SV = 2688, 2944, 3200
COL_AB = 3456
D_PROJ = 3584

VMEM_LIMIT_BYTES = 56 * 1024 * 1024


def _cparams(sem, vmem=None):
    return pltpu.CompilerParams(dimension_semantics=sem, vmem_limit_bytes=vmem)


def _rms(x, w):
    ms = jnp.mean(x * x, axis=-1, keepdims=True)
    return x * lax.rsqrt(ms + RMS_EPS) * w


def _proj_kernel(h_ref, nw_ref, w_ref, o_ref, *, tn):
    hn = _rms(h_ref[...], nw_ref[...]).astype(BF16)
    n = w_ref.shape[1]
    for c in range(n // tn):
        o_ref[:, c * tn:(c + 1) * tn] = jnp.dot(
            hn, w_ref[:, c * tn:(c + 1) * tn], preferred_element_type=F32).astype(o_ref.dtype)


def norm_proj(h, nw, w, *, tm=512, tn=512):
    t, d = h.shape
    n = w.shape[1]
    return pl.pallas_call(
        functools.partial(_proj_kernel, tn=tn),
        grid=(t // tm,),
        in_specs=[pl.BlockSpec((tm, d), lambda i: (i, 0)),
                  pl.BlockSpec((1, d), lambda i: (0, 0)),
                  pl.BlockSpec((d, n), lambda i: (0, 0))],
        out_specs=pl.BlockSpec((tm, n), lambda i: (i, 0)),
        out_shape=jax.ShapeDtypeStruct((t, n), BF16),
        compiler_params=_cparams(("arbitrary",), VMEM_LIMIT_BYTES),
        name="norm_proj",
    )(h, nw.reshape(1, d), w)


def _mix_out_kernel(h_ref, a_ref, b_ref, c_ref, w_ref, o_ref):
    acc = jnp.dot(a_ref[...], w_ref[0:D_GDN, :], preferred_element_type=F32)
    acc += jnp.dot(b_ref[...], w_ref[D_GDN:D_GDN + D_DSW, :], preferred_element_type=F32)
    acc += jnp.dot(c_ref[...], w_ref[D_GDN + D_DSW:, :], preferred_element_type=F32)
    o_ref[...] = h_ref[...] + acc


def mix_out(h, o_gdn, o_dsw, o_sb, w_out, *, tm=512):
    t, d = h.shape
    return pl.pallas_call(
        _mix_out_kernel,
        grid=(t // tm,),
        in_specs=[pl.BlockSpec((tm, d), lambda i: (i, 0)),
                  pl.BlockSpec((tm, D_GDN), lambda i: (i, 0)),
                  pl.BlockSpec((tm, D_DSW), lambda i: (i, 0)),
                  pl.BlockSpec((tm, D_SB), lambda i: (i, 0)),
                  pl.BlockSpec((d, d), lambda i: (0, 0))],
        out_specs=pl.BlockSpec((tm, d), lambda i: (i, 0)),
        out_shape=jax.ShapeDtypeStruct((t, d), F32),
        compiler_params=_cparams(("arbitrary",), VMEM_LIMIT_BYTES),
        name="mix_out",
    )(h, o_gdn, o_dsw, o_sb, w_out)


def _ffn_kernel(h_ref, nw_ref, w1_ref, w3_ref, w2_ref, o_ref, *, tf):
    h = h_ref[...]
    hn = _rms(h, nw_ref[...]).astype(BF16)
    acc = h
    for c in range(w1_ref.shape[1] // tf):
        sl = slice(c * tf, (c + 1) * tf)
        a = jnp.dot(hn, w1_ref[:, sl], preferred_element_type=F32)
        b = jnp.dot(hn, w3_ref[:, sl], preferred_element_type=F32)
        g = (a * jax.nn.sigmoid(a) * b).astype(BF16)
        acc = acc + jnp.dot(g, w2_ref[sl, :], preferred_element_type=F32)
    o_ref[...] = acc


def ffn_dense(h, nw, w1, w3, w2, *, tm=512, tf=704):
    t, d = h.shape
    f = w1.shape[1]
    const = dict(pipeline_mode=pl.Buffered(1))
    return pl.pallas_call(
        functools.partial(_ffn_kernel, tf=tf),
        grid=(t // tm,),
        in_specs=[pl.BlockSpec((tm, d), lambda i: (i, 0)),
                  pl.BlockSpec((1, d), lambda i: (0, 0)),
                  pl.BlockSpec((d, f), lambda i: (0, 0), **const),
                  pl.BlockSpec((d, f), lambda i: (0, 0), **const),
                  pl.BlockSpec((f, d), lambda i: (0, 0), **const)],
        out_specs=pl.BlockSpec((tm, d), lambda i: (i, 0)),
        out_shape=jax.ShapeDtypeStruct((t, d), F32),
        compiler_params=_cparams(("arbitrary",), VMEM_LIMIT_BYTES),
        name="ffn_dense",
    )(h, nw.reshape(1, d), w1, w3, w2)


def _ple_kernel(*refs, moe, final):
    h_ref = refs[0]
    p_ref, proj_ref, nw_ref, gate_ref, fw_ref, o_ref = refs[-6:]
    h = h_ref[...]
    if moe:
        y0_ref, y1_ref, g_ref = refs[1:4]
        g = g_ref[...]
        h = h + g[:, 0:1] * y0_ref[...] + g[:, HEAD_DIM:HEAD_DIM + 1] * y1_ref[...]
    hn = _rms(h, nw_ref[...]).astype(BF16)
    gate = jax.nn.sigmoid(jnp.dot(hn, gate_ref[...], preferred_element_type=F32))
    emb = jnp.dot(p_ref[...].astype(BF16), proj_ref[...], preferred_element_type=F32)
    out = h + emb * gate
    if final:
        out = _rms(out, fw_ref[...])
    o_ref[...] = out


def ple_layer(h, moe_out, p, proj, nw, gate, final_w, *, final, tm=512):
    t, d = h.shape
    in_specs = [pl.BlockSpec((tm, d), lambda i: (i, 0))]
    args = [h]
    if moe_out is not None:
        y, gates = moe_out
        in_specs += [pl.BlockSpec((tm, d), lambda i: (i, 0)),
                     pl.BlockSpec((tm, d), lambda i: (t // tm + i, 0)),
                     pl.BlockSpec((tm, LANES), lambda i: (i, 0))]
        args += [y, y, gates]
    in_specs += [pl.BlockSpec((tm, PLE_DIM), lambda i: (i, 0)),
                 pl.BlockSpec((PLE_DIM, d), lambda i: (0, 0)),
                 pl.BlockSpec((1, d), lambda i: (0, 0)),
                 pl.BlockSpec((d, d), lambda i: (0, 0)),
                 pl.BlockSpec((1, d), lambda i: (0, 0))]
    args += [p, proj, nw.reshape(1, d), gate, final_w.reshape(1, d)]
    return pl.pallas_call(
        functools.partial(_ple_kernel, moe=moe_out is not None, final=final),
        grid=(t // tm,),
        in_specs=in_specs,
        out_specs=pl.BlockSpec((tm, d), lambda i: (i, 0)),
        out_shape=jax.ShapeDtypeStruct((t, d), F32),
        compiler_params=_cparams(("arbitrary",), VMEM_LIMIT_BYTES),
        name="ple_layer",
    )(*args)


def _lane_iota(shape):
    return lax.broadcasted_iota(jnp.int32, shape, len(shape) - 1)


def _head_sumsq(x):
    lo = _lane_iota(x.shape) < HEAD_DIM
    sq = x * x
    s_lo = jnp.sum(jnp.where(lo, sq, 0.0), axis=-1, keepdims=True)
    s_hi = jnp.sum(jnp.where(lo, 0.0, sq), axis=-1, keepdims=True)
    return jnp.where(lo, s_lo, s_hi)


def _head_rms(x, w2):
    return x * lax.rsqrt(_head_sumsq(x) * (1.0 / HEAD_DIM) + RMS_EPS) * w2


def _stack_heads(x):
    lo = _lane_iota(x.shape) < HEAD_DIM
    zero = jnp.zeros_like(x)
    return jnp.concatenate([jnp.where(lo, x, zero), jnp.where(lo, zero, x)], axis=0)


def _round_robin(gens):
    gens = list(gens)
    while gens:
        alive = []
        for g in gens:
            try:
                next(g)
                alive.append(g)
            except StopIteration:
                pass
        gens = alive


def _dot(a, b):
    return jnp.dot(a, b, preferred_element_type=F32)


def _dot_nt(a, b):
    return lax.dot_general(a, b, (((1,), (1,)), ((), ())), preferred_element_type=F32)


def _dot_tn(a, b):
    return lax.dot_general(a, b, (((0,), (0,)), ((), ())), preferred_element_type=F32)


def _sb_kernel(q_ref, k_ref, v_ref, nw_ref, o_ref, *, tq):
    i = pl.program_id(2)
    lane = _lane_iota((1, LANES))
    row = lax.broadcasted_iota(jnp.int32, (tq, tq), 0)
    col = lax.broadcasted_iota(jnp.int32, (tq, tq), 1)
    upper = jnp.where(row > col, 1.0, 0.0).astype(BF16)
    diag_before = col < row
    q = q_ref[0] * jnp.asarray(HEAD_DIM ** -0.5, BF16)
    zero = jnp.zeros_like(q)
    qh = (jnp.where(lane < HEAD_DIM, q, zero), jnp.where(lane < HEAD_DIM, zero, q))

    def body(carry):
        jj, _, heads = carry
        j = i - jj
        start = pl.multiple_of(j * tq, tq)
        kj = k_ref[0, pl.ds(start, tq), :]
        vj = v_ref[0, pl.ds(start, tq), :]
        before = jnp.logical_or(diag_before, jj > 0)
        out = [None, None]

        def head_chain(hh):
            acc, c = heads[hh]
            yield
            z = _dot_nt(qh[hh], kj)
            sp = jnp.maximum(z, 0.0) + jnp.log(1.0 + jnp.exp(-jnp.abs(z)))
            lk = jnp.where(before, -sp, 0.0)
            lk_hi = lk.astype(BF16)
            lk_lo = (lk - lk_hi.astype(F32)).astype(BF16)
            yield
            suffix = _dot(lk_hi, upper) + _dot(lk_lo, upper)
            a = jnp.where(before, jnp.exp(z - sp + suffix + c), 0.0).astype(BF16)
            yield
            out[hh] = (acc + _dot(a, vj), c + jnp.sum(lk, axis=-1, keepdims=True))

        _round_robin([head_chain(hh) for hh in range(2)])
        cmax = jnp.maximum(jnp.max(out[0][1]), jnp.max(out[1][1]))
        return jj + 1, cmax, tuple(out)

    def cond(carry):
        jj, cmax, _ = carry
        return jnp.logical_and(jj <= i, cmax > SB_EXIT_LOG)

    init = tuple((jnp.zeros((tq, LANES), F32), jnp.zeros((tq, 1), F32)) for _ in range(2))
    _, _, ((acc0, _), (acc1, _)) = lax.while_loop(cond, body, (jnp.int32(0), jnp.float32(0.0), init))
    o = jnp.where(lane < HEAD_DIM, acc0, acc1)
    o_ref[0] = _head_rms(o, nw_ref[...]).astype(o_ref.dtype)


def sb_attention(proj, norm_w, *, tq=256):
    b, s, _ = proj.shape
    cq, ck, cv = COL_SQ // LANES, COL_SK // LANES, COL_SV // LANES
    nw2 = jnp.tile(norm_w.astype(F32), 2).reshape(1, LANES)
    return pl.pallas_call(
        functools.partial(_sb_kernel, tq=tq),
        grid=(b, N_PAIRS_SB, s // tq),
        in_specs=[pl.BlockSpec((1, tq, LANES), lambda bi, p, i: (bi, i, cq + p)),
                  pl.BlockSpec((1, s, LANES), lambda bi, p, i: (bi, 0, ck + p)),
                  pl.BlockSpec((1, s, LANES), lambda bi, p, i: (bi, 0, cv + p)),
                  pl.BlockSpec((1, LANES), lambda bi, p, i: (0, 0))],
        out_specs=pl.BlockSpec((1, tq, LANES), lambda bi, p, i: (bi, i, p)),
        out_shape=jax.ShapeDtypeStruct((b, s, D_SB), BF16),
        compiler_params=_cparams(("arbitrary",) * 3, VMEM_LIMIT_BYTES),
        name="sb_attention",
    )(proj, proj, proj, nw2)


DSW_PERM_ROWS = 1024


def _dsw_kernel(slope_ref, q_ref, k_ref, v_ref, dist_ref, nw_ref, o_ref,
                qf, kf, vf, qp, kp, vp, op, lp, oacc, lacc, *, tq):
    hp = pl.program_id(1)
    seq = q_ref.shape[1]
    lane = _lane_iota((1, LANES))
    lo = lane < HEAD_DIM
    nperm = seq // DSW_PERM_ROWS

    def to_f32(m, _):
        rows = pl.ds(pl.multiple_of(m * DSW_PERM_ROWS, DSW_PERM_ROWS), DSW_PERM_ROWS)
        for src, dst in ((q_ref, qf), (k_ref, kf), (v_ref, vf)):
            dst[rows, :] = src[0, rows, :].astype(F32)
        return 0

    lax.fori_loop(0, nperm, to_f32, 0)

    def attend(qs, ks, vs, o_dst, l_dst, r):
        length = seq // r
        tb = min(tq, length)
        nblk = length // tb
        slopes = [slope_ref[2 * hp + hh] * float(r) for hh in range(2)]

        def head_chain(hh, q, kc, vc, dist, res):
            zero = jnp.zeros_like(q)
            qh = jnp.where(lo, q, zero) if hh == 0 else jnp.where(lo, zero, q)
            yield
            sc = _dot_nt(qh, kc) - slopes[hh] * dist
            m = jnp.max(sc, axis=-1, keepdims=True)
            e = jnp.exp(sc - m)
            den = jnp.sum(e, axis=-1, keepdims=True)
            eb = e.astype(BF16)
            yield
            res[hh] = (_dot(eb, vc) / den, m + jnp.log(den))

        def block_chains(it):
            n = it % nblk
            start = pl.multiple_of(it * tb, tb)
            pstart = pl.multiple_of(jnp.maximum(start - DSW_SPAN, 0), DSW_SPAN)
            q = qs[pl.ds(start, tb), :] * jnp.asarray(HEAD_DIM ** -0.5, BF16)
            kc = jnp.concatenate([ks[pl.ds(pstart, DSW_SPAN), :], ks[pl.ds(start, tb), :]], axis=0)
            vc = jnp.concatenate([vs[pl.ds(pstart, DSW_SPAN), :], vs[pl.ds(start, tb), :]], axis=0)
            dist = dist_ref[jnp.minimum(n, 1)]
            res = [None, None]
            return start, res, [head_chain(hh, q, kc, vc, dist, res) for hh in range(2)]

        def body(it2, _):
            blocks = [block_chains(2 * it2 + u) for u in range(2)]
            _round_robin([g for _, _, gens in blocks for g in gens])
            for start, res, _ in blocks:
                o_dst[pl.ds(start, tb), :] = jnp.where(lo, res[0][0], res[1][0])
                l_dst[pl.ds(start, tb), :] = jnp.where(lo, res[0][1], res[1][1])
            return 0

        lax.fori_loop(0, r * nblk // 2, body, 0)

    for _, r in DSW_CONFIGS:
        if r == 1:
            attend(q_ref.at[0], k_ref.at[0], v_ref.at[0], oacc, lacc, 1)
            continue
        length = seq // r
        per = DSW_PERM_ROWS // r

        def permute(m, _, r=r, length=length, per=per):
            base = pl.multiple_of(m * DSW_PERM_ROWS, DSW_PERM_ROWS)
            for c in range(r):
                dst = pl.ds(pl.multiple_of(c * length + m * per, per), per)
                for src, dstref in ((qf, qp), (kf, kp), (vf, vp)):
                    dstref[dst, :] = src.at[pl.ds(base, DSW_PERM_ROWS)][pl.ds(c, per, stride=r), :].astype(BF16)
            return 0

        lax.fori_loop(0, nperm, permute, 0)
        attend(qp, kp, vp, op, lp, r)

        def merge(m, _, r=r, length=length, per=per):
            base = pl.multiple_of(m * DSW_PERM_ROWS, DSW_PERM_ROWS)
            for c in range(r):
                src = pl.ds(pl.multiple_of(c * length + m * per, per), per)
                rows = pl.ds(c, per, stride=r)
                o_view = oacc.at[pl.ds(base, DSW_PERM_ROWS)]
                l_view = lacc.at[pl.ds(base, DSW_PERM_ROWS)]
                o_old, l_old = o_view[rows, :], l_view[rows, :]
                o_new, l_new = op[src, :], lp[src, :]
                l_max = jnp.maximum(l_old, l_new)
                w_old, w_new = jnp.exp(l_old - l_max), jnp.exp(l_new - l_max)
                tot = w_old + w_new
                o_view[rows, :] = (w_old * o_old + w_new * o_new) / tot
                l_view[rows, :] = l_max + jnp.log(tot)
            return 0

        lax.fori_loop(0, nperm, merge, 0)

    def finish(m, _):
        rows = pl.ds(pl.multiple_of(m * DSW_PERM_ROWS, DSW_PERM_ROWS), DSW_PERM_ROWS)
        o_ref[0, rows, :] = _head_rms(oacc[rows, :], nw_ref[...]).astype(o_ref.dtype)
        return 0

    lax.fori_loop(0, nperm, finish, 0)


def _dsw_dist_table(tq):
    qi = np.arange(tq)[:, None]
    ki = np.arange(tq + DSW_SPAN)[None, :]
    dist = qi - ki + DSW_SPAN
    valid = (dist >= 0) & (dist <= DSW_SPAN)
    later = np.where(valid, dist, 1e30).astype(np.float32)
    first = np.where(valid & (ki >= DSW_SPAN), dist, 1e30).astype(np.float32)
    return jnp.asarray(np.stack([first, later]))


def dsw_mixer(proj, slopes, norm_w, *, tq=256):
    b, s, _ = proj.shape
    cq, ck, cv = COL_DQ // LANES, COL_DK // LANES, COL_DV // LANES
    nw2 = jnp.tile(norm_w.astype(F32), 2).reshape(1, LANES)
    dist = _dsw_dist_table(tq)

    def col(base):
        return pl.BlockSpec((1, s, LANES), lambda bi, p: (bi, 0, base + p))

    return pl.pallas_call(
        functools.partial(_dsw_kernel, tq=tq),
        grid=(b, N_PAIRS_DSW),
        in_specs=[pl.BlockSpec(memory_space=pltpu.SMEM),
                  col(cq), col(ck), col(cv),
                  pl.BlockSpec((2, tq, tq + DSW_SPAN), lambda bi, p: (0, 0, 0)),
                  pl.BlockSpec((1, LANES), lambda bi, p: (0, 0))],
        out_specs=pl.BlockSpec((1, s, LANES), lambda bi, p: (bi, 0, p)),
        out_shape=jax.ShapeDtypeStruct((b, s, D_DSW), BF16),
        scratch_shapes=[pltpu.VMEM((s, LANES), F32)] * 3
        + [pltpu.VMEM((s, LANES), BF16)] * 3
        + [pltpu.VMEM((s, LANES), F32)] * 4,
        compiler_params=_cparams(("arbitrary",) * 2, VMEM_LIMIT_BYTES),
        name="dsw_mixer",
    )(slopes, proj, proj, proj, dist, nw2)


def _gdn_kernel(xq_ref, xk_ref, xv_ref, z_ref, ab_ref, cw_ref, expand_ref, alog_ref, dtb_ref, nw_ref, o_ref,
                u_s, w_s, qd_s, kd_s, qk_s, gl_s, *, group):
    c = GDN_CHUNK
    seq = xq_ref.shape[1]
    nchunk = seq // c
    lane = _lane_iota((c, LANES))
    rowi = lax.broadcasted_iota(jnp.int32, (c, LANES), 0)
    colj = jnp.where(lane < HEAD_DIM, lane, lane - HEAD_DIM)
    causal = rowi >= colj
    strict = rowi > colj
    eye_cat = jnp.where(rowi == colj, 1.0, 0.0)
    r2 = lax.broadcasted_iota(jnp.int32, (LANES, LANES), 0)
    c2 = lax.broadcasted_iota(jnp.int32, (LANES, LANES), 1)
    same_head = (r2 < HEAD_DIM) == (c2 < HEAD_DIM)
    nw2 = nw_ref[...]

    def conv_silu(x_ref, which, p, start, has_prev):
        lanes = slice(p * LANES, (p + 1) * LANES)
        w = cw_ref[:, which * D_GDN + p * LANES:which * D_GDN + (p + 1) * LANES]
        x = x_ref[0, pl.ds(start, c), lanes].astype(F32)
        pstart = pl.multiple_of(jnp.maximum(start - 16, 0), 16)
        prev = x_ref[0, pl.ds(pstart, 16), lanes].astype(F32)[8:]
        xx = jnp.concatenate([jnp.where(has_prev, prev, 0.0), x], axis=0)
        y = xx[8:] * w[3:4]
        for s in (1, 2, 3):
            y = y + pltpu.roll(xx, s, axis=0)[8:] * w[3 - s:4 - s]
        return y * jax.nn.sigmoid(y)

    def chunk_local(p, n):
        start = pl.multiple_of(n * c, c)
        has_prev = n > 0
        q = conv_silu(xq_ref, 0, p, start, has_prev)
        k = conv_silu(xk_ref, 1, p, start, has_prev)
        v = conv_silu(xv_ref, 2, p, start, has_prev)
        q = q * lax.rsqrt(_head_sumsq(q) + RMS_EPS) * (HEAD_DIM ** -0.5)
        k = k * lax.rsqrt(_head_sumsq(k) + RMS_EPS)
        ab = _dot(ab_ref[0, pl.ds(start, c), :], expand_ref[p])
        a_e, b_e = ab[:, :LANES], ab[:, LANES:]
        xg = a_e + dtb_ref[p]
        g = -jnp.exp(alog_ref[p]) * (jnp.maximum(xg, 0.0) + jnp.log(1.0 + jnp.exp(-jnp.abs(xg))))
        beta = jax.nn.sigmoid(b_e)
        gcol = g
        for s in (1, 2, 4, 8, 16, 32):
            gcol = gcol + jnp.where(rowi >= s, pltpu.roll(gcol, s, axis=0), 0.0)
        grow = jnp.sum(jnp.where(rowi == colj, gcol, 0.0), axis=0, keepdims=True)
        glast = gcol[c - 1:c, :]
        diff = gcol - grow
        decay = jnp.exp(jnp.where(causal, diff, NEG_BIG))
        eg = jnp.exp(gcol)
        kb = k.astype(BF16)
        ks = _stack_heads(kb)
        lhs = jnp.concatenate([kb, q.astype(BF16)], axis=0)
        rhs = jnp.concatenate([_stack_heads((v * beta).astype(BF16)),
                               _stack_heads((k * beta * eg).astype(BF16))], axis=1)
        yield
        kq = _dot_nt(lhs, ks)
        kk_cat, qk_cat = kq[:c], kq[c:]
        pm = jnp.where(strict, -(beta * kk_cat * decay), 0.0)
        t = eye_cat + pm
        pb = pm.astype(BF16)
        pbs = _stack_heads(pb)
        yield
        pm = _dot(pb, pbs)
        for _ in range(4):
            pb = pm.astype(BF16)
            lhs = jnp.concatenate([t.astype(BF16), pb], axis=0)
            pbs = _stack_heads(pb)
            yield
            res = _dot(lhs, pbs)
            t = t + res[:c]
            pm = res[c:]
        pb = pm.astype(BF16)
        tb = t.astype(BF16)
        pbs = _stack_heads(pb)
        yield
        t = t + _dot(tb, pbs)
        tb = t.astype(BF16)
        yield
        uw = _dot(tb, rhs)
        u, w = uw[:, :LANES], uw[:, LANES:]
        rows = pl.ds(start, c)
        u_s[p, rows, :] = u
        w_s[p, rows, :] = w.astype(BF16)
        qd_s[p, rows, :] = (q * eg).astype(BF16)
        kd_s[p, rows, :] = (k * jnp.exp(glast - gcol)).astype(BF16)
        qk_s[p, rows, :] = jnp.where(causal, qk_cat * decay, 0.0).astype(BF16)
        gl_s[p, n] = jnp.broadcast_to(jnp.exp(glast), (8, LANES))

    for p in range(N_PAIRS_GDN):
        def local_body(m, _, p=p):
            _round_robin([chunk_local(p, m * group + gi) for gi in range(group)])
            return 0
        lax.fori_loop(0, nchunk // group, local_body, 0)

    def state_step(p, n, state, result):
        start = pl.multiple_of(n * c, c)
        rows = pl.ds(start, c)
        lanes = slice(p * LANES, (p + 1) * LANES)
        lhs = jnp.concatenate([w_s[p, rows, :], qd_s[p, rows, :]], axis=0)
        sb = state.astype(BF16)
        yield
        ws = _dot(lhs, sb)
        v_new = u_s[p, rows, :] - ws[:c]
        vb = v_new.astype(BF16)
        vbs = _stack_heads(vb)
        yield
        o = ws[c:] + _dot(qk_s[p, rows, :], vbs)
        upd = _dot_tn(kd_s[p, rows, :], vb)
        result[p] = state * gl_s[p, n][0:1] + jnp.where(same_head, upd, 0.0)
        zg = z_ref[0, rows, lanes].astype(F32)
        out = _head_rms(o, nw2) * (zg * jax.nn.sigmoid(zg))
        o_ref[0, rows, lanes] = out.astype(o_ref.dtype)

    def state_body(n, states):
        result = [None] * N_PAIRS_GDN
        _round_robin([state_step(p, n, states[p], result) for p in range(N_PAIRS_GDN)])
        return tuple(result)

    lax.fori_loop(0, nchunk, state_body, tuple(jnp.zeros((LANES, LANES), F32) for _ in range(N_PAIRS_GDN)))


def gdn_mixer(proj, conv_w, a_log, dt_bias, norm_w):
    b, s, _ = proj.shape
    npair = N_PAIRS_GDN
    ex = np.zeros((npair, LANES, 2 * LANES), np.float32)
    for p in range(npair):
        for hh in range(2):
            ex[p, 2 * p + hh, hh * HEAD_DIM:(hh + 1) * HEAD_DIM] = 1.0
            ex[p, 2 * npair + 2 * p + hh, LANES + hh * HEAD_DIM:LANES + (hh + 1) * HEAD_DIM] = 1.0
    expand = jnp.asarray(ex, BF16)
    alog2 = jnp.repeat(a_log.astype(F32).reshape(npair, 1, 2), HEAD_DIM, axis=2)
    dtb2 = jnp.repeat(dt_bias.astype(F32).reshape(npair, 1, 2), HEAD_DIM, axis=2)
    nw2 = jnp.tile(norm_w.astype(F32), 2).reshape(1, LANES)
    cw = conv_w.astype(F32)
    once = dict(pipeline_mode=pl.Buffered(1))

    def xcol(base):
        return pl.BlockSpec((1, s, D_GDN), lambda bi: (bi, 0, base // D_GDN), **once)

    def whole(shape):
        return pl.BlockSpec(shape, lambda bi: (0,) * len(shape))

    return pl.pallas_call(
        functools.partial(_gdn_kernel, group=8),
        grid=(b,),
        in_specs=[xcol(COL_GQ), xcol(COL_GK), xcol(COL_GV), xcol(COL_GZ),
                  pl.BlockSpec((1, s, LANES), lambda bi: (bi, 0, COL_AB // LANES), **once),
                  whole((CONV_WIDTH, 3 * D_GDN)),
                  whole((npair, LANES, 2 * LANES)),
                  whole((npair, 1, LANES)),
                  whole((npair, 1, LANES)),
                  whole((1, LANES))],
        out_specs=pl.BlockSpec((1, s, D_GDN), lambda bi: (bi, 0, 0)),
        out_shape=jax.ShapeDtypeStruct((b, s, D_GDN), BF16),
        scratch_shapes=[pltpu.VMEM((npair, s, LANES), F32)]
        + [pltpu.VMEM((npair, s, LANES), BF16)] * 4
        + [pltpu.VMEM((npair, s // GDN_CHUNK, 8, LANES), F32)],
        compiler_params=_cparams(("arbitrary",), VMEM_LIMIT_BYTES),
        name="gdn_mixer",
    )(proj, proj, proj, proj, proj, cw, expand, alog2, dtb2, nw2)


def _router_kernel(h_ref, nw_ref, rw_ref, hn_ref, idx_ref, gate_ref):
    hn = _rms(h_ref[...], nw_ref[...])
    hn_ref[...] = hn
    lt = lax.dot_general(rw_ref[...], hn, (((1,), (1,)), ((), ())),
                         preferred_element_type=F32, precision=lax.Precision.HIGHEST)
    row = lax.broadcasted_iota(jnp.int32, lt.shape, 0)
    v1 = jnp.max(lt, axis=0, keepdims=True)
    i1 = jnp.min(jnp.where(lt == v1, row, N_EXPERTS), axis=0, keepdims=True)
    lt2 = jnp.where(row == i1, -jnp.inf, lt)
    v2 = jnp.max(lt2, axis=0, keepdims=True)
    i2 = jnp.min(jnp.where(lt2 == v2, row, N_EXPERTS), axis=0, keepdims=True)
    e = jnp.exp(v2 - v1)
    g1 = 1.0 / (1.0 + e)
    idx_ref[...] = jnp.concatenate([i1, i2], axis=0)
    g8 = jnp.concatenate([g1, e * g1, jnp.zeros((N_EXPERTS - 2, g1.shape[1]), F32)], axis=0)
    srow = lax.broadcasted_iota(jnp.int32, (N_EXPERTS, LANES), 0)
    slane = lax.broadcasted_iota(jnp.int32, (N_EXPERTS, LANES), 1)
    sel = jnp.where(srow == slane // HEAD_DIM, 1.0, 0.0)
    gate_ref[...] = lax.dot_general(g8, sel, (((0,), (0,)), ((), ())),
                                    preferred_element_type=F32, precision=lax.Precision.HIGHEST)


def moe_router(h, nw, router_w, *, tm=512):
    t, d = h.shape
    return pl.pallas_call(
        _router_kernel,
        grid=(t // tm,),
        in_specs=[pl.BlockSpec((tm, d), lambda i: (i, 0)),
                  pl.BlockSpec((1, d), lambda i: (0, 0)),
                  pl.BlockSpec((N_EXPERTS, d), lambda i: (0, 0))],
        out_specs=[pl.BlockSpec((tm, d), lambda i: (i, 0)),
                   pl.BlockSpec((2, tm), lambda i: (0, i)),
                   pl.BlockSpec((tm, LANES), lambda i: (i, 0))],
        out_shape=[jax.ShapeDtypeStruct((t, d), F32),
                   jax.ShapeDtypeStruct((2, t), jnp.int32),
                   jax.ShapeDtypeStruct((t, LANES), F32)],
        compiler_params=_cparams(("arbitrary",), VMEM_LIMIT_BYTES),
        name="moe_router",
    )(h, nw.reshape(1, d), router_w.astype(F32).T)


def _moe_kernel(te_ref, tv_ref, tok_ref, tokn_ref, dstp_ref, dst_ref, hn_hbm, w1_ref, w3_ref, w2_ref, y_hbm,
                xbuf, xb, acc, ybuf, gsem, ssem, *, tm, nf):
    i = pl.program_id(0)
    f = pl.program_id(1)
    nt = pl.num_programs(0)
    valid = tv_ref[i] > 0
    chunk = tm // (nf - 1)

    def gather_row(tref, r):
        return pltpu.make_async_copy(hn_hbm.at[pl.ds(tref[0, 0, r], 1)], xbuf.at[pl.ds(r, 1)], gsem)

    def scatter_row(dref, r):
        return pltpu.make_async_copy(ybuf.at[pl.ds(r, 1)], y_hbm.at[pl.ds(dref[0, 0, r], 1)], ssem)

    def gather_wait():
        pltpu.make_async_copy(xbuf, xbuf, gsem).wait()

    def scatter_wait():
        pltpu.make_async_copy(ybuf, ybuf, ssem).wait()

    def issue_chunk():
        base = f * chunk
        for k in range(chunk):
            gather_row(tokn_ref, base + k).start()
            scatter_row(dstp_ref, base + k).start()

    def expert_slice():
        x = xb[...]
        a = _dot(x, w1_ref[0, 0])
        b = _dot(x, w3_ref[0, 0])
        g = (a * jax.nn.sigmoid(a) * b).astype(BF16)
        acc[...] += _dot(g, w2_ref[0])

    @pl.when(jnp.logical_and(i == 0, f == 0))
    def _():
        ybuf[...] = jnp.zeros_like(ybuf)
        tail = pltpu.make_async_copy(ybuf, y_hbm.at[pl.ds(y_hbm.shape[0] - tm, tm)], ssem)
        tail.start()
        tail.wait()

        def body(r, _):
            gather_row(tok_ref, r).start()
            return 0
        lax.fori_loop(0, tm, body, 0, unroll=8)

    @pl.when(f == 0)
    def _():
        gather_wait()
        xb[...] = xbuf[...].astype(BF16)
        acc[...] = jnp.zeros_like(acc)

    @pl.when(jnp.logical_and(valid, f < nf - 1))
    def _():
        expert_slice()
        issue_chunk()

    @pl.when(jnp.logical_and(valid, f == nf - 1))
    def _():
        expert_slice()

    @pl.when(jnp.logical_and(jnp.logical_not(valid), f < nf - 1))
    def _():
        issue_chunk()

    @pl.when(f == nf - 1)
    def _():
        scatter_wait()
        ybuf[...] = acc[...]

    @pl.when(jnp.logical_and(i == nt - 1, f == nf - 1))
    def _():
        def body(r, _):
            scatter_row(dst_ref, r).start()
            return 0
        lax.fori_loop(0, tm, body, 0, unroll=8)
        scatter_wait()
        gather_wait()


def moe_experts(hn, idx, w1, w3, w2, *, tm=768):
    t, d = hn.shape
    nf, tf = w1.shape[1], w1.shape[3]
    assert tm % (nf - 1) == 0 and (tm // (nf - 1)) % 8 == 0
    na = 2 * t
    nt = -(-na // tm) + N_EXPERTS
    rmax = nt * tm
    e_flat = idx.reshape(na)
    onehot = (e_flat[:, None] == jnp.arange(N_EXPERTS)[None, :]).astype(jnp.int32)
    csum = jnp.cumsum(onehot, axis=0)
    counts = csum[-1]
    rank = jnp.sum(csum * onehot, axis=1) - 1
    padded = ((counts + tm - 1) // tm) * tm
    ends = jnp.cumsum(padded)
    offs = ends - padded
    dest = offs[e_flat] + rank
    a_ids = jnp.arange(na, dtype=jnp.int32)
    row_asg = jnp.full((rmax,), -1, jnp.int32).at[dest].set(a_ids, unique_indices=True)
    row_real = row_asg >= 0
    row_tok = jnp.where(row_real, row_asg % t, 0)
    row_dst = jnp.where(row_real, row_asg, na + jnp.arange(rmax, dtype=jnp.int32) % tm)
    tile_start = jnp.arange(nt, dtype=jnp.int32) * tm
    tile_valid = (tile_start < ends[-1]).astype(jnp.int32)
    tile_exp = jnp.minimum(jnp.sum((tile_start[:, None] >= ends[None, :]).astype(jnp.int32), axis=1),
                           N_EXPERTS - 1).astype(jnp.int32)
    last_e = tile_exp[jnp.maximum(ends[-1] // tm - 1, 0)]
    tile_exp = jnp.where(tile_valid > 0, tile_exp, last_e)
    row_tok = row_tok.reshape(nt, 1, tm)
    row_dst = row_dst.reshape(nt, 1, tm)

    def frozen_f(i, f, tv):
        return jnp.where(tv[i] > 0, f, nf - 1)

    def rows_of(tile_of):
        return pl.BlockSpec((1, 1, tm), lambda i, f, te, tv: (tile_of(i), 0, 0), memory_space=pltpu.SMEM)

    grid_spec = pltpu.PrefetchScalarGridSpec(
        num_scalar_prefetch=2,
        grid=(nt, nf),
        in_specs=[rows_of(lambda i: i),
                  rows_of(lambda i: jnp.minimum(i + 1, nt - 1)),
                  rows_of(lambda i: jnp.maximum(i - 1, 0)),
                  rows_of(lambda i: i),
                  pl.BlockSpec(memory_space=pl.ANY),
                  pl.BlockSpec((1, 1, d, tf), lambda i, f, te, tv: (te[i], frozen_f(i, f, tv), 0, 0)),
                  pl.BlockSpec((1, 1, d, tf), lambda i, f, te, tv: (te[i], frozen_f(i, f, tv), 0, 0)),
                  pl.BlockSpec((1, tf, d), lambda i, f, te, tv: (te[i], frozen_f(i, f, tv), 0))],
        out_specs=pl.BlockSpec(memory_space=pl.ANY),
        scratch_shapes=[pltpu.VMEM((tm, d), F32),
                        pltpu.VMEM((tm, d), BF16),
                        pltpu.VMEM((tm, d), F32),
                        pltpu.VMEM((tm, d), F32),
                        pltpu.SemaphoreType.DMA(()),
                        pltpu.SemaphoreType.DMA(())],
    )
    return pl.pallas_call(
        functools.partial(_moe_kernel, tm=tm, nf=nf),
        grid_spec=grid_spec,
        out_shape=jax.ShapeDtypeStruct((na + tm, d), F32),
        compiler_params=_cparams(("arbitrary",) * 2, VMEM_LIMIT_BYTES),
        name="moe_experts",
    )(tile_exp, tile_valid, row_tok, row_tok, row_dst, row_dst, hn, w1, w3, w2)


MOE_TF = 512


def _tile_expert_cols(w):
    e, d, ff = w.shape
    return w.astype(BF16).reshape(e, d, ff // MOE_TF, MOE_TF).transpose(0, 2, 1, 3)


def moe_layer(h, nw, router_w, w1, w3, w2):
    hn, idx, gates = moe_router(h, nw, router_w)
    return moe_experts(hn, idx, _tile_expert_cols(w1), _tile_expert_cols(w3), w2.astype(BF16)), gates


def _pack_w_in(w):
    d = w.shape[0]
    n_ab = 2 * N_PAIRS_GDN * 2
    gdn_main = 4 * D_GDN
    rest = w[:, gdn_main + n_ab:]
    ab = jnp.pad(w[:, gdn_main:gdn_main + n_ab], ((0, 0), (0, LANES - n_ab)))
    return jnp.concatenate([w[:, :gdn_main], rest, ab], axis=1).astype(BF16)


def kernel(x, p, attn_norm_w, w_in, conv_w, a_log, dt_bias, gdn_norm_w, dsw_norm_w, sb_norm_w, w_out,
           ffn_norm_w, ffn_w1, ffn_w3, ffn_w2, router_w, moe_w1, moe_w3, moe_w2, ple_proj, ple_norm_w,
           ple_gate, final_norm_w):
    b, s, d = x.shape
    depth = w_in.shape[0]
    t = b * s
    h = x.reshape(t, d)
    n_dsw_heads = 2 * N_PAIRS_DSW
    slopes = jnp.asarray([2.0 ** (-8.0 * (i + 1) / n_dsw_heads) for i in range(n_dsw_heads)], F32)
    for i in range(depth):
        proj = norm_proj(h, attn_norm_w[i], _pack_w_in(w_in[i])).reshape(b, s, D_PROJ)
        o_gdn = gdn_mixer(proj, conv_w[i], a_log[i], dt_bias[i], gdn_norm_w[i]).reshape(t, D_GDN)
        o_dsw = dsw_mixer(proj, slopes, dsw_norm_w[i]).reshape(t, D_DSW)
        o_sb = sb_attention(proj, sb_norm_w[i]).reshape(t, D_SB)
        h = mix_out(h, o_gdn, o_dsw, o_sb, w_out[i].astype(BF16))
        j = i // 2
        if i % 2 == 0:
            h = ffn_dense(h, ffn_norm_w[i], ffn_w1[j].astype(BF16), ffn_w3[j].astype(BF16),
                          ffn_w2[j].astype(BF16))
            moe_out = None
        else:
            moe_out = moe_layer(h, ffn_norm_w[i], router_w[j], moe_w1[j], moe_w3[j], moe_w2[j])
        h = ple_layer(h, moe_out, p[i].reshape(t, PLE_DIM), ple_proj[i].astype(BF16), ple_norm_w[i],
                      ple_gate[i].astype(BF16), final_norm_w, final=(i == depth - 1))
    return h.reshape(b, s, d)
```

```python
import functools
import math

import jax
import jax.numpy as jnp
import numpy as np
from jax import lax
from jax.experimental import pallas as pl
from jax.experimental.pallas import tpu as pltpu

F32 = jnp.float32
BF16 = jnp.bfloat16

D_MODEL = 1024
HEAD_DIM = 64
LANES = 128
N_PAIRS_GDN = 3
N_PAIRS_DSW = 3
N_PAIRS_SB = 2
D_GDN = 384
D_DSW = 384
D_SB = 256
CONV_WIDTH = 4
GDN_CHUNK = 64
DSW_CONFIGS = ((128, 1), (512, 4), (2048, 16))
DSW_SPAN = 128
D_FF = 2816
N_EXPERTS = 8
MOE_D_FF = 3584
PLE_DIM = 256
RMS_EPS = 1e-6
NEG_BIG = -1e30
SB_EXIT_LOG = -110.0

COL_GQ, COL_GK, COL_GV, COL_GZ = 0, 384, 768, 1152
COL_DQ, COL_DK, COL_DV = 1536, 1920, 2304
COL_SQ, COL_SK, COL_SV = 2688, 2944, 3200
COL_AB = 3456
D_PROJ = 3584

VMEM_LIMIT_BYTES = 56 * 1024 * 1024


def _cparams(sem, vmem=None):
    return pltpu.CompilerParams(dimension_semantics=sem, vmem_limit_bytes=vmem)


def _rms(x, w):
    ms = jnp.mean(x * x, axis=-1, keepdims=True)
    return x * lax.rsqrt(ms + RMS_EPS) * w


def _proj_kernel(h_ref, nw_ref, w_ref, o_ref, *, tn):
    hn = _rms(h_ref[...], nw_ref[...]).astype(BF16)
    n = w_ref.shape[1]
    for c in range(n // tn):
        o_ref[:, c * tn:(c + 1) * tn] = jnp.dot(
            hn, w_ref[:, c * tn:(c + 1) * tn], preferred_element_type=F32).astype(o_ref.dtype)


def norm_proj(h, nw, w, *, tm=512, tn=512):
    t, d = h.shape
    n = w.shape[1]
    return pl.pallas_call(
        functools.partial(_proj_kernel, tn=tn),
        grid=(t // tm,),
        in_specs=[pl.BlockSpec((tm, d), lambda i: (i, 0)),
                  pl.BlockSpec((1, d), lambda i: (0, 0)),
                  pl.BlockSpec((d, n), lambda i: (0, 0))],
        out_specs=pl.BlockSpec((tm, n), lambda i: (i, 0)),
        out_shape=jax.ShapeDtypeStruct((t, n), BF16),
        compiler_params=_cparams(("arbitrary",), VMEM_LIMIT_BYTES),
        name="norm_proj",
    )(h, nw.reshape(1, d), w)


def _mix_out_kernel(h_ref, a_ref, b_ref, c_ref, w_ref, o_ref):
    acc = jnp.dot(a_ref[...], w_ref[0:D_GDN, :], preferred_element_type=F32)
    acc += jnp.dot(b_ref[...], w_ref[D_GDN:D_GDN + D_DSW, :], preferred_element_type=F32)
    acc += jnp.dot(c_ref[...], w_ref[D_GDN + D_DSW:, :], preferred_element_type=F32)
    o_ref[...] = h_ref[...] + acc


def mix_out(h, o_gdn, o_dsw, o_sb, w_out, *, tm=512):
    t, d = h.shape
    return pl.pallas_call(
        _mix_out_kernel,
        grid=(t // tm,),
        in_specs=[pl.BlockSpec((tm, d), lambda i: (i, 0)),
                  pl.BlockSpec((tm, D_GDN), lambda i: (i, 0)),
                  pl.BlockSpec((tm, D_DSW), lambda i: (i, 0)),
                  pl.BlockSpec((tm, D_SB), lambda i: (i, 0)),
                  pl.BlockSpec((d, d), lambda i: (0, 0))],
        out_specs=pl.BlockSpec((tm, d), lambda i: (i, 0)),
        out_shape=jax.ShapeDtypeStruct((t, d), F32),
        compiler_params=_cparams(("arbitrary",), VMEM_LIMIT_BYTES),
        name="mix_out",
    )(h, o_gdn, o_dsw, o_sb, w_out)


def _ffn_kernel(h_ref, nw_ref, w1_ref, w3_ref, w2_ref, o_ref, *, tf):
    h = h_ref[...]
    hn = _rms(h, nw_ref[...]).astype(BF16)
    acc = h
    for c in range(w1_ref.shape[1] // tf):
        sl = slice(c * tf, (c + 1) * tf)
        a = jnp.dot(hn, w1_ref[:, sl], preferred_element_type=F32)
        b = jnp.dot(hn, w3_ref[:, sl], preferred_element_type=F32)
        g = (a * jax.nn.sigmoid(a) * b).astype(BF16)
        acc = acc + jnp.dot(g, w2_ref[sl, :], preferred_element_type=F32)
    o_ref[...] = acc


def ffn_dense(h, nw, w1, w3, w2, *, tm=512, tf=704):
    t, d = h.shape
    f = w1.shape[1]
    const = dict(pipeline_mode=pl.Buffered(1))
    return pl.pallas_call(
        functools.partial(_ffn_kernel, tf=tf),
        grid=(t // tm,),
        in_specs=[pl.BlockSpec((tm, d), lambda i: (i, 0)),
                  pl.BlockSpec((1, d), lambda i: (0, 0)),
                  pl.BlockSpec((d, f), lambda i: (0, 0), **const),
                  pl.BlockSpec((d, f), lambda i: (0, 0), **const),
                  pl.BlockSpec((f, d), lambda i: (0, 0), **const)],
        out_specs=pl.BlockSpec((tm, d), lambda i: (i, 0)),
        out_shape=jax.ShapeDtypeStruct((t, d), F32),
        compiler_params=_cparams(("arbitrary",), VMEM_LIMIT_BYTES),
        name="ffn_dense",
    )(h, nw.reshape(1, d), w1, w3, w2)


def _ple_kernel(*refs, moe, final):
    h_ref = refs[0]
    p_ref, proj_ref, nw_ref, gate_ref, fw_ref, o_ref = refs[-6:]
    h = h_ref[...]
    if moe:
        y0_ref, y1_ref, g_ref = refs[1:4]
        g = g_ref[...]
        h = h + g[:, 0:1] * y0_ref[...] + g[:, HEAD_DIM:HEAD_DIM + 1] * y1_ref[...]
    hn = _rms(h, nw_ref[...]).astype(BF16)
    gate = jax.nn.sigmoid(jnp.dot(hn, gate_ref[...], preferred_element_type=F32))
    emb = jnp.dot(p_ref[...].astype(BF16), proj_ref[...], preferred_element_type=F32)
    out = h + emb * gate
    if final:
        out = _rms(out, fw_ref[...])
    o_ref[...] = out


def ple_layer(h, moe_out, p, proj, nw, gate, final_w, *, final, tm=512):
    t, d = h.shape
    in_specs = [pl.BlockSpec((tm, d), lambda i: (i, 0))]
    args = [h]
    if moe_out is not None:
        y, gates = moe_out
        in_specs += [pl.BlockSpec((tm, d), lambda i: (i, 0)),
                     pl.BlockSpec((tm, d), lambda i: (t // tm + i, 0)),
                     pl.BlockSpec((tm, LANES), lambda i: (i, 0))]
        args += [y, y, gates]
    in_specs += [pl.BlockSpec((tm, PLE_DIM), lambda i: (i, 0)),
                 pl.BlockSpec((PLE_DIM, d), lambda i: (0, 0)),
                 pl.BlockSpec((1, d), lambda i: (0, 0)),
                 pl.BlockSpec((d, d), lambda i: (0, 0)),
                 pl.BlockSpec((1, d), lambda i: (0, 0))]
    args += [p, proj, nw.reshape(1, d), gate, final_w.reshape(1, d)]
    return pl.pallas_call(
        functools.partial(_ple_kernel, moe=moe_out is not None, final=final),
        grid=(t // tm,),
        in_specs=in_specs,
        out_specs=pl.BlockSpec((tm, d), lambda i: (i, 0)),
        out_shape=jax.ShapeDtypeStruct((t, d), F32),
        compiler_params=_cparams(("arbitrary",), VMEM_LIMIT_BYTES),
        name="ple_layer",
    )(*args)


def _lane_iota(shape):
    return lax.broadcasted_iota(jnp.int32, shape, len(shape) - 1)


def _head_sumsq(x):
    lo = _lane_iota(x.shape) < HEAD_DIM
    sq = x * x
    s_lo = jnp.sum(jnp.where(lo, sq, 0.0), axis=-1, keepdims=True)
    s_hi = jnp.sum(jnp.where(lo, 0.0, sq), axis=-1, keepdims=True)
    return jnp.where(lo, s_lo, s_hi)


def _head_rms(x, w2):
    return x * lax.rsqrt(_head_sumsq(x) * (1.0 / HEAD_DIM) + RMS_EPS) * w2


def _stack_heads(x):
    lo = _lane_iota(x.shape) < HEAD_DIM
    zero = jnp.zeros_like(x)
    return jnp.concatenate([jnp.where(lo, x, zero), jnp.where(lo, zero, x)], axis=0)


def _round_robin(gens):
    gens = list(gens)
    while gens:
        alive = []
        for g in gens:
            try:
                next(g)
                alive.append(g)
            except StopIteration:
                pass
        gens = alive


def _dot(a, b):
    return jnp.dot(a, b, preferred_element_type=F32)


def _dot_nt(a, b):
    return lax.dot_general(a, b, (((1,), (1,)), ((), ())), preferred_element_type=F32)


def _dot_tn(a, b):
    return lax.dot_general(a, b, (((0,), (0,)), ((), ())), preferred_element_type=F32)


def _sb_kernel(q_ref, k_ref, v_ref, nw_ref, o_ref, *, tq):
    i = pl.program_id(2)
    lane = _lane_iota((1, LANES))
    row = lax.broadcasted_iota(jnp.int32, (tq, tq), 0)
    col = lax.broadcasted_iota(jnp.int32, (tq, tq), 1)
    upper = jnp.where(row > col, 1.0, 0.0).astype(BF16)
    diag_before = col < row
    q = q_ref[0] * jnp.asarray(HEAD_DIM ** -0.5, BF16)
    zero = jnp.zeros_like(q)
    qh = (jnp.where(lane < HEAD_DIM, q, zero), jnp.where(lane < HEAD_DIM, zero, q))

    def body(carry):
        jj, _, heads = carry
        j = i - jj
        start = pl.multiple_of(j * tq, tq)
        kj = k_ref[0, pl.ds(start, tq), :]
        vj = v_ref[0, pl.ds(start, tq), :]
        before = jnp.logical_or(diag_before, jj > 0)
        out = [None, None]

        def head_chain(hh):
            acc, c = heads[hh]
            yield
            z = _dot_nt(qh[hh], kj)
            sp = jnp.maximum(z, 0.0) + jnp.log(1.0 + jnp.exp(-jnp.abs(z)))
            lk = jnp.where(before, -sp, 0.0)
            lk_hi = lk.astype(BF16)
            lk_lo = (lk - lk_hi.astype(F32)).astype(BF16)
            yield
            suffix = _dot(lk_hi, upper) + _dot(lk_lo, upper)
            a = jnp.where(before, jnp.exp(z - sp + suffix + c), 0.0).astype(BF16)
            yield
            out[hh] = (acc + _dot(a, vj), c + jnp.sum(lk, axis=-1, keepdims=True))

        _round_robin([head_chain(hh) for hh in range(2)])
        cmax = jnp.maximum(jnp.max(out[0][1]), jnp.max(out[1][1]))
        return jj + 1, cmax, tuple(out)

    def cond(carry):
        jj, cmax, _ = carry
        return jnp.logical_and(jj <= i, cmax > SB_EXIT_LOG)

    init = tuple((jnp.zeros((tq, LANES), F32), jnp.zeros((tq, 1), F32)) for _ in range(2))
    _, _, ((acc0, _), (acc1, _)) = lax.while_loop(cond, body, (jnp.int32(0), jnp.float32(0.0), init))
    o = jnp.where(lane < HEAD_DIM, acc0, acc1)
    o_ref[0] = _head_rms(o, nw_ref[...]).astype(o_ref.dtype)


def sb_attention(proj, norm_w, *, tq=256):
    b, s, _ = proj.shape
    cq, ck, cv = COL_SQ // LANES, COL_SK // LANES, COL_SV // LANES
    nw2 = jnp.tile(norm_w.astype(F32), 2).reshape(1, LANES)
    return pl.pallas_call(
        functools.partial(_sb_kernel, tq=tq),
        grid=(b, N_PAIRS_SB, s // tq),
        in_specs=[pl.BlockSpec((1, tq, LANES), lambda bi, p, i: (bi, i, cq + p)),
                  pl.BlockSpec((1, s, LANES), lambda bi, p, i: (bi, 0, ck + p)),
                  pl.BlockSpec((1, s, LANES), lambda bi, p, i: (bi, 0, cv + p)),
                  pl.BlockSpec((1, LANES), lambda bi, p, i: (0, 0))],
        out_specs=pl.BlockSpec((1, tq, LANES), lambda bi, p, i: (bi, i, p)),
        out_shape=jax.ShapeDtypeStruct((b, s, D_SB), BF16),
        compiler_params=_cparams(("arbitrary",) * 3, VMEM_LIMIT_BYTES),
        name="sb_attention",
    )(proj, proj, proj, nw2)


DSW_PERM_ROWS = 1024


def _dsw_kernel(slope_ref, q_ref, k_ref, v_ref, dist_ref, nw_ref, o_ref,
                qf, kf, vf, qp, kp, vp, op, lp, oacc, lacc, *, tq):
    hp = pl.program_id(1)
    seq = q_ref.shape[1]
    lane = _lane_iota((1, LANES))
    lo = lane < HEAD_DIM
    nperm = seq // DSW_PERM_ROWS

    def to_f32(m, _):
        rows = pl.ds(pl.multiple_of(m * DSW_PERM_ROWS, DSW_PERM_ROWS), DSW_PERM_ROWS)
        for src, dst in ((q_ref, qf), (k_ref, kf), (v_ref, vf)):
            dst[rows, :] = src[0, rows, :].astype(F32)
        return 0

    lax.fori_loop(0, nperm, to_f32, 0)

    def attend(qs, ks, vs, o_dst, l_dst, r):
        length = seq // r
        tb = min(tq, length)
        nblk = length // tb
        slopes = [slope_ref[2 * hp + hh] * float(r) for hh in range(2)]

        def head_chain(hh, q, kc, vc, dist, res):
            zero = jnp.zeros_like(q)
            qh = jnp.where(lo, q, zero) if hh == 0 else jnp.where(lo, zero, q)
            yield
            sc = _dot_nt(qh, kc) - slopes[hh] * dist
            m = jnp.max(sc, axis=-1, keepdims=True)
            e = jnp.exp(sc - m)
            den = jnp.sum(e, axis=-1, keepdims=True)
            eb = e.astype(BF16)
            yield
            res[hh] = (_dot(eb, vc) / den, m + jnp.log(den))

        def block_chains(it):
            n = it % nblk
            start = pl.multiple_of(it * tb, tb)
            pstart = pl.multiple_of(jnp.maximum(start - DSW_SPAN, 0), DSW_SPAN)
            q = qs[pl.ds(start, tb), :] * jnp.asarray(HEAD_DIM ** -0.5, BF16)
            kc = jnp.concatenate([ks[pl.ds(pstart, DSW_SPAN), :], ks[pl.ds(start, tb), :]], axis=0)
            vc = jnp.concatenate([vs[pl.ds(pstart, DSW_SPAN), :], vs[pl.ds(start, tb), :]], axis=0)
            dist = dist_ref[jnp.minimum(n, 1)]
            res = [None, None]
            return start, res, [head_chain(hh, q, kc, vc, dist, res) for hh in range(2)]

        def body(it2, _):
            blocks = [block_chains(2 * it2 + u) for u in range(2)]
            _round_robin([g for _, _, gens in blocks for g in gens])
            for start, res, _ in blocks:
                o_dst[pl.ds(start, tb), :] = jnp.where(lo, res[0][0], res[1][0])
                l_dst[pl.ds(start, tb), :] = jnp.where(lo, res[0][1], res[1][1])
            return 0

        lax.fori_loop(0, r * nblk // 2, body, 0)

    for _, r in DSW_CONFIGS:
        if r == 1:
            attend(q_ref.at[0], k_ref.at[0], v_ref.at[0], oacc, lacc, 1)
            continue
        length = seq // r
        per = DSW_PERM_ROWS // r

        def permute(m, _, r=r, length=length, per=per):
            base = pl.multiple_of(m * DSW_PERM_ROWS, DSW_PERM_ROWS)
            for c in range(r):
                dst = pl.ds(pl.multiple_of(c * length + m * per, per), per)
                for src, dstref in ((qf, qp), (kf, kp), (vf, vp)):
                    dstref[dst, :] = src.at[pl.ds(base, DSW_PERM_ROWS)][pl.ds(c, per, stride=r), :].astype(BF16)
            return 0

        lax.fori_loop(0, nperm, permute, 0)
        attend(qp, kp, vp, op, lp, r)

        def merge(m, _, r=r, length=length, per=per):
            base = pl.multiple_of(m * DSW_PERM_ROWS, DSW_PERM_ROWS)
            for c in range(r):
                src = pl.ds(pl.multiple_of(c * length + m * per, per), per)
                rows = pl.ds(c, per, stride=r)
                o_view = oacc.at[pl.ds(base, DSW_PERM_ROWS)]
                l_view = lacc.at[pl.ds(base, DSW_PERM_ROWS)]
                o_old, l_old = o_view[rows, :], l_view[rows, :]
                o_new, l_new = op[src, :], lp[src, :]
                l_max = jnp.maximum(l_old, l_new)
                w_old, w_new = jnp.exp(l_old - l_max), jnp.exp(l_new - l_max)
                tot = w_old + w_new
                o_view[rows, :] = (w_old * o_old + w_new * o_new) / tot
                l_view[rows, :] = l_max + jnp.log(tot)
            return 0

        lax.fori_loop(0, nperm, merge, 0)

    def finish(m, _):
        rows = pl.ds(pl.multiple_of(m * DSW_PERM_ROWS, DSW_PERM_ROWS), DSW_PERM_ROWS)
        o_ref[0, rows, :] = _head_rms(oacc[rows, :], nw_ref[...]).astype(o_ref.dtype)
        return 0

    lax.fori_loop(0, nperm, finish, 0)


def _dsw_dist_table(tq):
    qi = np.arange(tq)[:, None]
    ki = np.arange(tq + DSW_SPAN)[None, :]
    dist = qi - ki + DSW_SPAN
    valid = (dist >= 0) & (dist <= DSW_SPAN)
    later = np.where(valid, dist, 1e30).astype(np.float32)
    first = np.where(valid & (ki >= DSW_SPAN), dist, 1e30).astype(np.float32)
    return jnp.asarray(np.stack([first, later]))


def dsw_mixer(proj, slopes, norm_w, *, tq=256):
    b, s, _ = proj.shape
    cq, ck, cv = COL_DQ // LANES, COL_DK // LANES, COL_DV // LANES
    nw2 = jnp.tile(norm_w.astype(F32), 2).reshape(1, LANES)
    dist = _dsw_dist_table(tq)

    def col(base):
        return pl.BlockSpec((1, s, LANES), lambda bi, p: (bi, 0, base + p))

    return pl.pallas_call(
        functools.partial(_dsw_kernel, tq=tq),
        grid=(b, N_PAIRS_DSW),
        in_specs=[pl.BlockSpec(memory_space=pltpu.SMEM),
                  col(cq), col(ck), col(cv),
                  pl.BlockSpec((2, tq, tq + DSW_SPAN), lambda bi, p: (0, 0, 0)),
                  pl.BlockSpec((1, LANES), lambda bi, p: (0, 0))],
        out_specs=pl.BlockSpec((1, s, LANES), lambda bi, p: (bi, 0, p)),
        out_shape=jax.ShapeDtypeStruct((b, s, D_DSW), BF16),
        scratch_shapes=[pltpu.VMEM((s, LANES), F32)] * 3
        + [pltpu.VMEM((s, LANES), BF16)] * 3
        + [pltpu.VMEM((s, LANES), F32)] * 4,
        compiler_params=_cparams(("arbitrary",) * 2, VMEM_LIMIT_BYTES),
        name="dsw_mixer",
    )(slopes, proj, proj, proj, dist, nw2)


def _gdn_kernel(xq_ref, xk_ref, xv_ref, z_ref, ab_ref, cw_ref, expand_ref, alog_ref, dtb_ref, nw_ref, o_ref,
                m_s, n_s, a_s, b_s, gl_s, *, group):
    c = GDN_CHUNK
    seq = xq_ref.shape[1]
    nchunk = seq // c
    lane = _lane_iota((c, LANES))
    rowi = lax.broadcasted_iota(jnp.int32, (c, LANES), 0)
    colj = jnp.where(lane < HEAD_DIM, lane, lane - HEAD_DIM)
    causal = rowi >= colj
    strict = rowi > colj
    eye_cat = jnp.where(rowi == colj, 1.0, 0.0)
    r2 = lax.broadcasted_iota(jnp.int32, (LANES, LANES), 0)
    c2 = lax.broadcasted_iota(jnp.int32, (LANES, LANES), 1)
    same_head = (r2 < HEAD_DIM) == (c2 < HEAD_DIM)
    nw2 = nw_ref[...]

    def conv_silu(x_ref, which, p, start, has_prev):
        lanes = slice(p * LANES, (p + 1) * LANES)
        w = cw_ref[:, which * D_GDN + p * LANES:which * D_GDN + (p + 1) * LANES]
        x = x_ref[0, pl.ds(start, c), lanes].astype(F32)
        pstart = pl.multiple_of(jnp.maximum(start - 16, 0), 16)
        prev = x_ref[0, pl.ds(pstart, 16), lanes].astype(F32)[8:]
        xx = jnp.concatenate([jnp.where(has_prev, prev, 0.0), x], axis=0)
        y = xx[8:] * w[3:4]
        for s in (1, 2, 3):
            y = y + pltpu.roll(xx, s, axis=0)[8:] * w[3 - s:4 - s]
        return y * jax.nn.sigmoid(y)

    def chunk_local(p, n):
        start = pl.multiple_of(n * c, c)
        has_prev = n > 0
        q = conv_silu(xq_ref, 0, p, start, has_prev)
        k = conv_silu(xk_ref, 1, p, start, has_prev)
        v = conv_silu(xv_ref, 2, p, start, has_prev)
        q = q * lax.rsqrt(_head_sumsq(q) + RMS_EPS) * (HEAD_DIM ** -0.5)
        k = k * lax.rsqrt(_head_sumsq(k) + RMS_EPS)
        ab = _dot(ab_ref[0, pl.ds(start, c), :], expand_ref[p])
        a_e, b_e = ab[:, :LANES], ab[:, LANES:]
        xg = a_e + dtb_ref[p]
        g = -jnp.exp(alog_ref[p]) * (jnp.maximum(xg, 0.0) + jnp.log(1.0 + jnp.exp(-jnp.abs(xg))))
        beta = jax.nn.sigmoid(b_e)
        gcol = g
        for s in (1, 2, 4, 8, 16, 32):
            gcol = gcol + jnp.where(rowi >= s, pltpu.roll(gcol, s, axis=0), 0.0)
        grow = jnp.sum(jnp.where(rowi == colj, gcol, 0.0), axis=0, keepdims=True)
        glast = gcol[c - 1:c, :]
        diff = gcol - grow
        decay = jnp.exp(jnp.where(causal, diff, NEG_BIG))
        eg = jnp.exp(gcol)
        kb = k.astype(BF16)
        ks = _stack_heads(kb)
        lhs = jnp.concatenate([kb, q.astype(BF16)], axis=0)
        rhs = jnp.concatenate([_stack_heads((v * beta).astype(BF16)),
                               _stack_heads((k * beta * eg).astype(BF16))], axis=1)
        yield
        kq = _dot_nt(lhs, ks)
        kk_cat, qk_cat = kq[:c], kq[c:]
        pm = jnp.where(strict, -(beta * kk_cat * decay), 0.0)
        t = eye_cat + pm
        pb = pm.astype(BF16)
        pbs = _stack_heads(pb)
        yield
        pm = _dot(pb, pbs)
        for _ in range(4):
            pb = pm.astype(BF16)
            lhs = jnp.concatenate([t.astype(BF16), pb], axis=0)
            pbs = _stack_heads(pb)
            yield
            res = _dot(lhs, pbs)
            t = t + res[:c]
            pm = res[c:]
        pb = pm.astype(BF16)
        tb = t.astype(BF16)
        pbs = _stack_heads(pb)
        yield
        t = t + _dot(tb, pbs)
        tb = t.astype(BF16)
        yield
        uw = _dot(tb, rhs)
        wub = uw.astype(BF16)
        kdb = (k * jnp.exp(glast - gcol)).astype(BF16)
        qkb = jnp.where(causal, qk_cat * decay, 0.0).astype(BF16)
        wus = jnp.concatenate([_stack_heads(wub[:, :LANES]), _stack_heads(wub[:, LANES:])], axis=1)
        yield
        kuw = _dot_tn(kdb, wub)
        yield
        quw = _dot(qkb, wus)
        rows = pl.ds(start, c)
        m_s[p, n] = jnp.where(same_head, -kuw[:, LANES:], 0.0).astype(BF16)
        n_s[p, n] = jnp.where(same_head, kuw[:, :LANES], 0.0).astype(BF16)
        a_s[p, rows, :] = (q * eg - quw[:, LANES:]).astype(BF16)
        b_s[p, rows, :] = quw[:, :LANES].astype(BF16)
        gl_s[p, n] = jnp.broadcast_to(jnp.exp(glast), (8, LANES))

    for p in range(N_PAIRS_GDN):
        def local_body(m, _, p=p):
            _round_robin([chunk_local(p, m * group + gi) for gi in range(group)])
            return 0
        lax.fori_loop(0, nchunk // group, local_body, 0)

    def state_step(p, n, state, result):
        start = pl.multiple_of(n * c, c)
        rows = pl.ds(start, c)
        lanes = slice(p * LANES, (p + 1) * LANES)
        lhs = jnp.concatenate([m_s[p, n], a_s[p, rows, :]], axis=0)
        sb = state.astype(BF16)
        yield
        res = _dot(lhs, sb)
        result[p] = state * gl_s[p, n][0:1] + res[:LANES] + n_s[p, n].astype(F32)
        o = res[LANES:] + b_s[p, rows, :].astype(F32)
        zg = z_ref[0, rows, lanes].astype(F32)
        out = _head_rms(o, nw2) * (zg * jax.nn.sigmoid(zg))
        o_ref[0, rows, lanes] = out.astype(o_ref.dtype)

    def state_body(n, states):
        result = [None] * N_PAIRS_GDN
        _round_robin([state_step(p, n, states[p], result) for p in range(N_PAIRS_GDN)])
        return tuple(result)

    lax.fori_loop(0, nchunk, state_body, tuple(jnp.zeros((LANES, LANES), F32) for _ in range(N_PAIRS_GDN)))


def gdn_mixer(proj, conv_w, a_log, dt_bias, norm_w):
    b, s, _ = proj.shape
    npair = N_PAIRS_GDN
    ex = np.zeros((npair, LANES, 2 * LANES), np.float32)
    for p in range(npair):
        for hh in range(2):
            ex[p, 2 * p + hh, hh * HEAD_DIM:(hh + 1) * HEAD_DIM] = 1.0
            ex[p, 2 * npair + 2 * p + hh, LANES + hh * HEAD_DIM:LANES + (hh + 1) * HEAD_DIM] = 1.0
    expand = jnp.asarray(ex, BF16)
    alog2 = jnp.repeat(a_log.astype(F32).reshape(npair, 1, 2), HEAD_DIM, axis=2)
    dtb2 = jnp.repeat(dt_bias.astype(F32).reshape(npair, 1, 2), HEAD_DIM, axis=2)
    nw2 = jnp.tile(norm_w.astype(F32), 2).reshape(1, LANES)
    cw = conv_w.astype(F32)
    once = dict(pipeline_mode=pl.Buffered(1))

    def xcol(base):
        return pl.BlockSpec((1, s, D_GDN), lambda bi: (bi, 0, base // D_GDN), **once)

    def whole(shape):
        return pl.BlockSpec(shape, lambda bi: (0,) * len(shape))

    return pl.pallas_call(
        functools.partial(_gdn_kernel, group=8),
        grid=(b,),
        in_specs=[xcol(COL_GQ), xcol(COL_GK), xcol(COL_GV), xcol(COL_GZ),
                  pl.BlockSpec((1, s, LANES), lambda bi: (bi, 0, COL_AB // LANES), **once),
                  whole((CONV_WIDTH, 3 * D_GDN)),
                  whole((npair, LANES, 2 * LANES)),
                  whole((npair, 1, LANES)),
                  whole((npair, 1, LANES)),
                  whole((1, LANES))],
        out_specs=pl.BlockSpec((1, s, D_GDN), lambda bi: (bi, 0, 0)),
        out_shape=jax.ShapeDtypeStruct((b, s, D_GDN), BF16),
        scratch_shapes=[pltpu.VMEM((npair, s // GDN_CHUNK, LANES, LANES), BF16)] * 2
        + [pltpu.VMEM((npair, s, LANES), BF16)] * 2
        + [pltpu.VMEM((npair, s // GDN_CHUNK, 8, LANES), F32)],
        compiler_params=_cparams(("arbitrary",), VMEM_LIMIT_BYTES),
        name="gdn_mixer",
    )(proj, proj, proj, proj, proj, cw, expand, alog2, dtb2, nw2)


def _router_kernel(h_ref, nw_ref, rw_ref, hn_ref, idx_ref, gate_ref):
    hn = _rms(h_ref[...], nw_ref[...])
    hn_ref[...] = hn
    lt = lax.dot_general(rw_ref[...], hn, (((1,), (1,)), ((), ())),
                         preferred_element_type=F32, precision=lax.Precision.HIGHEST)
    row = lax.broadcasted_iota(jnp.int32, lt.shape, 0)
    v1 = jnp.max(lt, axis=0, keepdims=True)
    i1 = jnp.min(jnp.where(lt == v1, row, N_EXPERTS), axis=0, keepdims=True)
    lt2 = jnp.where(row == i1, -jnp.inf, lt)
    v2 = jnp.max(lt2, axis=0, keepdims=True)
    i2 = jnp.min(jnp.where(lt2 == v2, row, N_EXPERTS), axis=0, keepdims=True)
    e = jnp.exp(v2 - v1)
    g1 = 1.0 / (1.0 + e)
    idx_ref[...] = jnp.concatenate([i1, i2], axis=0)
    g8 = jnp.concatenate([g1, e * g1, jnp.zeros((N_EXPERTS - 2, g1.shape[1]), F32)], axis=0)
    srow = lax.broadcasted_iota(jnp.int32, (N_EXPERTS, LANES), 0)
    slane = lax.broadcasted_iota(jnp.int32, (N_EXPERTS, LANES), 1)
    sel = jnp.where(srow == slane // HEAD_DIM, 1.0, 0.0)
    gate_ref[...] = lax.dot_general(g8, sel, (((0,), (0,)), ((), ())),
                                    preferred_element_type=F32, precision=lax.Precision.HIGHEST)


def moe_router(h, nw, router_w, *, tm=512):
    t, d = h.shape
    return pl.pallas_call(
        _router_kernel,
        grid=(t // tm,),
        in_specs=[pl.BlockSpec((tm, d), lambda i: (i, 0)),
                  pl.BlockSpec((1, d), lambda i: (0, 0)),
                  pl.BlockSpec((N_EXPERTS, d), lambda i: (0, 0))],
        out_specs=[pl.BlockSpec((tm, d), lambda i: (i, 0)),
                   pl.BlockSpec((2, tm), lambda i: (0, i)),
                   pl.BlockSpec((tm, LANES), lambda i: (i, 0))],
        out_shape=[jax.ShapeDtypeStruct((t, d), F32),
                   jax.ShapeDtypeStruct((2, t), jnp.int32),
                   jax.ShapeDtypeStruct((t, LANES), F32)],
        compiler_params=_cparams(("arbitrary",), VMEM_LIMIT_BYTES),
        name="moe_router",
    )(h, nw.reshape(1, d), router_w.astype(F32).T)


ROW_DMA_PRIORITY = 1


def _moe_kernel(te_ref, tv_ref, tok_ref, tokn_ref, dstp_ref, dst_ref, hn_hbm, w1_ref, w3_ref, w2_ref, y_hbm,
                xbuf, xb, acc, ybuf, gsem, ssem, *, tm, nf):
    i = pl.program_id(0)
    f = pl.program_id(1)
    nt = pl.num_programs(0)
    valid = tv_ref[i] > 0
    chunk = tm // (nf - 1)

    def gather_row(tref, r):
        return pltpu.make_async_copy(hn_hbm.at[pl.ds(tref[0, 0, r], 1)], xbuf.at[pl.ds(r, 1)], gsem)

    def scatter_row(dref, r):
        return pltpu.make_async_copy(ybuf.at[pl.ds(r, 1)], y_hbm.at[pl.ds(dref[0, 0, r], 1)], ssem)

    def gather_wait():
        pltpu.make_async_copy(xbuf, xbuf, gsem).wait()

    def scatter_wait():
        pltpu.make_async_copy(ybuf, ybuf, ssem).wait()

    def issue_chunk():
        base = f * chunk
        for k in range(chunk):
            gather_row(tokn_ref, base + k).start(priority=ROW_DMA_PRIORITY)
            scatter_row(dstp_ref, base + k).start(priority=ROW_DMA_PRIORITY)

    def expert_slice():
        x = xb[...]
        a = _dot(x, w1_ref[0, 0])
        b = _dot(x, w3_ref[0, 0])
        g = (a * jax.nn.sigmoid(a) * b).astype(BF16)
        acc[...] += _dot(g, w2_ref[0])

    @pl.when(jnp.logical_and(i == 0, f == 0))
    def _():
        ybuf[...] = jnp.zeros_like(ybuf)
        tail = pltpu.make_async_copy(ybuf, y_hbm.at[pl.ds(y_hbm.shape[0] - tm, tm)], ssem)
        tail.start()
        tail.wait()

        def body(r, _):
            gather_row(tok_ref, r).start()
            return 0
        lax.fori_loop(0, tm, body, 0, unroll=8)

    @pl.when(f == 0)
    def _():
        gather_wait()
        xb[...] = xbuf[...].astype(BF16)
        acc[...] = jnp.zeros_like(acc)

    @pl.when(jnp.logical_and(valid, f < nf - 1))
    def _():
        expert_slice()
        issue_chunk()

    @pl.when(jnp.logical_and(valid, f == nf - 1))
    def _():
        expert_slice()

    @pl.when(jnp.logical_and(jnp.logical_not(valid), f < nf - 1))
    def _():
        issue_chunk()

    @pl.when(f == nf - 1)
    def _():
        scatter_wait()
        ybuf[...] = acc[...]

    @pl.when(jnp.logical_and(i == nt - 1, f == nf - 1))
    def _():
        def body(r, _):
            scatter_row(dst_ref, r).start()
            return 0
        lax.fori_loop(0, tm, body, 0, unroll=8)
        scatter_wait()
        gather_wait()


def moe_experts(hn, idx, w1, w3, w2, *, tm=768):
    t, d = hn.shape
    nf, tf = w1.shape[1], w1.shape[3]
    assert tm % (nf - 1) == 0 and (tm // (nf - 1)) % 8 == 0
    na = 2 * t
    nt = -(-na // tm) + N_EXPERTS
    rmax = nt * tm
    e_flat = idx.reshape(na)
    onehot = (e_flat[:, None] == jnp.arange(N_EXPERTS)[None, :]).astype(jnp.int32)
    csum = jnp.cumsum(onehot, axis=0)
    counts = csum[-1]
    rank = jnp.sum(csum * onehot, axis=1) - 1
    padded = ((counts + tm - 1) // tm) * tm
    ends = jnp.cumsum(padded)
    offs = ends - padded
    dest = offs[e_flat] + rank
    a_ids = jnp.arange(na, dtype=jnp.int32)
    row_asg = jnp.full((rmax,), -1, jnp.int32).at[dest].set(a_ids, unique_indices=True)
    row_real = row_asg >= 0
    row_tok = jnp.where(row_real, row_asg % t, 0)
    row_dst = jnp.where(row_real, row_asg, na + jnp.arange(rmax, dtype=jnp.int32) % tm)
    tile_start = jnp.arange(nt, dtype=jnp.int32) * tm
    tile_valid = (tile_start < ends[-1]).astype(jnp.int32)
    tile_exp = jnp.minimum(jnp.sum((tile_start[:, None] >= ends[None, :]).astype(jnp.int32), axis=1),
                           N_EXPERTS - 1).astype(jnp.int32)
    last_e = tile_exp[jnp.maximum(ends[-1] // tm - 1, 0)]
    tile_exp = jnp.where(tile_valid > 0, tile_exp, last_e)
    row_tok = row_tok.reshape(nt, 1, tm)
    row_dst = row_dst.reshape(nt, 1, tm)

    def frozen_f(i, f, tv):
        return jnp.where(tv[i] > 0, f, nf - 1)

    def rows_of(tile_of):
        return pl.BlockSpec((1, 1, tm), lambda i, f, te, tv: (tile_of(i), 0, 0), memory_space=pltpu.SMEM)

    grid_spec = pltpu.PrefetchScalarGridSpec(
        num_scalar_prefetch=2,
        grid=(nt, nf),
        in_specs=[rows_of(lambda i: i),
                  rows_of(lambda i: jnp.minimum(i + 1, nt - 1)),
                  rows_of(lambda i: jnp.maximum(i - 1, 0)),
                  rows_of(lambda i: i),
                  pl.BlockSpec(memory_space=pl.ANY),
                  pl.BlockSpec((1, 1, d, tf), lambda i, f, te, tv: (te[i], frozen_f(i, f, tv), 0, 0)),
                  pl.BlockSpec((1, 1, d, tf), lambda i, f, te, tv: (te[i], frozen_f(i, f, tv), 0, 0)),
                  pl.BlockSpec((1, tf, d), lambda i, f, te, tv: (te[i], frozen_f(i, f, tv), 0))],
        out_specs=pl.BlockSpec(memory_space=pl.ANY),
        scratch_shapes=[pltpu.VMEM((tm, d), F32),
                        pltpu.VMEM((tm, d), BF16),
                        pltpu.VMEM((tm, d), F32),
                        pltpu.VMEM((tm, d), F32),
                        pltpu.SemaphoreType.DMA(()),
                        pltpu.SemaphoreType.DMA(())],
    )
    return pl.pallas_call(
        functools.partial(_moe_kernel, tm=tm, nf=nf),
        grid_spec=grid_spec,
        out_shape=jax.ShapeDtypeStruct((na + tm, d), F32),
        compiler_params=_cparams(("arbitrary",) * 2, VMEM_LIMIT_BYTES),
        name="moe_experts",
    )(tile_exp, tile_valid, row_tok, row_tok, row_dst, row_dst, hn, w1, w3, w2)


MOE_TF = 512


def _tile_expert_cols(w):
    e, d, ff = w.shape
    return w.astype(BF16).reshape(e, d, ff // MOE_TF, MOE_TF).transpose(0, 2, 1, 3)


def moe_layer(h, nw, router_w, w1, w3, w2):
    hn, idx, gates = moe_router(h, nw, router_w)
    return moe_experts(hn, idx, _tile_expert_cols(w1), _tile_expert_cols(w3), w2.astype(BF16)), gates


def _pack_w_in(w):
    d = w.shape[0]
    n_ab = 2 * N_PAIRS_GDN * 2
    gdn_main = 4 * D_GDN
    rest = w[:, gdn_main + n_ab:]
    ab = jnp.pad(w[:, gdn_main:gdn_main + n_ab], ((0, 0), (0, LANES - n_ab)))
    return jnp.concatenate([w[:, :gdn_main], rest, ab], axis=1).astype(BF16)


def kernel(x, p, attn_norm_w, w_in, conv_w, a_log, dt_bias, gdn_norm_w, dsw_norm_w, sb_norm_w, w_out,
           ffn_norm_w, ffn_w1, ffn_w3, ffn_w2, router_w, moe_w1, moe_w3, moe_w2, ple_proj, ple_norm_w,
           ple_gate, final_norm_w):
    b, s, d = x.shape
    depth = w_in.shape[0]
    t = b * s
    h = x.reshape(t, d)
    n_dsw_heads = 2 * N_PAIRS_DSW
    slopes = jnp.asarray([2.0 ** (-8.0 * (i + 1) / n_dsw_heads) for i in range(n_dsw_heads)], F32)
    for i in range(depth):
        proj = norm_proj(h, attn_norm_w[i], _pack_w_in(w_in[i])).reshape(b, s, D_PROJ)
        o_gdn = gdn_mixer(proj, conv_w[i], a_log[i], dt_bias[i], gdn_norm_w[i]).reshape(t, D_GDN)
        o_dsw = dsw_mixer(proj, slopes, dsw_norm_w[i]).reshape(t, D_DSW)
        o_sb = sb_attention(proj, sb_norm_w[i]).reshape(t, D_SB)
        h = mix_out(h, o_gdn, o_dsw, o_sb, w_out[i].astype(BF16))
        j = i // 2
        if i % 2 == 0:
            h = ffn_dense(h, ffn_norm_w[i], ffn_w1[j].astype(BF16), ffn_w3[j].astype(BF16),
                          ffn_w2[j].astype(BF16))
            moe_out = None
        else:
            moe_out = moe_layer(h, ffn_norm_w[i], router_w[j], moe_w1[j], moe_w3[j], moe_w2[j])
        h = ple_layer(h, moe_out, p[i].reshape(t, PLE_DIM), ple_proj[i].astype(BF16), ple_norm_w[i],
                      ple_gate[i].astype(BF16), final_norm_w, final=(i == depth - 1))
    return h.reshape(b, s, d)
```

```python
import functools
import math

import jax
import jax.numpy as jnp
import numpy as np
from jax import lax
from jax.experimental import pallas as pl
from jax.experimental.pallas import tpu as pltpu

F32 = jnp.float32
BF16 = jnp.bfloat16

D_MODEL = 1024
HEAD_DIM = 64
LANES = 128
N_PAIRS_GDN = 3
N_PAIRS_DSW = 3
N_PAIRS_SB = 2
D_GDN = 384
D_DSW = 384
D_SB = 256
CONV_WIDTH = 4
GDN_CHUNK = 64
DSW_CONFIGS = ((128, 1), (512, 4), (2048, 16))
DSW_SPAN = 128
D_FF = 2816
N_EXPERTS = 8
MOE_D_FF = 3584
PLE_DIM = 256
RMS_EPS = 1e-6
NEG_BIG = -1e30
SB_EXIT_LOG = -110.0

COL_GQ, COL_GK, COL_GV, COL_GZ = 0, 384, 768, 1152
COL_DQ, COL_DK, COL_DV = 1536, 1920, 2304
COL_SQ, COL_SK, COL_SV = 2688, 2944, 3200
COL_AB = 3456
D_PROJ = 3584

VMEM_LIMIT_BYTES = 56 * 1024 * 1024


def _cparams(sem, vmem=None):
    return pltpu.CompilerParams(dimension_semantics=sem, vmem_limit_bytes=vmem)


def _rms(x, w):
    ms = jnp.mean(x * x, axis=-1, keepdims=True)
    return x * lax.rsqrt(ms + RMS_EPS) * w


def _proj_kernel(h_ref, nw_ref, w_ref, o_ref, *, tn):
    hn = _rms(h_ref[...], nw_ref[...]).astype(BF16)
    n = w_ref.shape[1]
    for c in range(n // tn):
        o_ref[:, c * tn:(c + 1) * tn] = jnp.dot(
            hn, w_ref[:, c * tn:(c + 1) * tn], preferred_element_type=F32).astype(o_ref.dtype)


def norm_proj(h, nw, w, *, tm=512, tn=512):
    t, d = h.shape
    n = w.shape[1]
    return pl.pallas_call(
        functools.partial(_proj_kernel, tn=tn),
        grid=(t // tm,),
        in_specs=[pl.BlockSpec((tm, d), lambda i: (i, 0)),
                  pl.BlockSpec((1, d), lambda i: (0, 0)),
                  pl.BlockSpec((d, n), lambda i: (0, 0))],
        out_specs=pl.BlockSpec((tm, n), lambda i: (i, 0)),
        out_shape=jax.ShapeDtypeStruct((t, n), BF16),
        compiler_params=_cparams(("arbitrary",), VMEM_LIMIT_BYTES),
        name="norm_proj",
    )(h, nw.reshape(1, d), w)


def _mixed_residual(h_ref, a_ref, b_ref, c_ref, w_ref):
    acc = jnp.dot(a_ref[...], w_ref[0:D_GDN, :], preferred_element_type=F32)
    acc += jnp.dot(b_ref[...], w_ref[D_GDN:D_GDN + D_DSW, :], preferred_element_type=F32)
    acc += jnp.dot(c_ref[...], w_ref[D_GDN + D_DSW:, :], preferred_element_type=F32)
    return h_ref[...] + acc


def _mix_out_kernel(h_ref, a_ref, b_ref, c_ref, w_ref, o_ref):
    o_ref[...] = _mixed_residual(h_ref, a_ref, b_ref, c_ref, w_ref)


def mix_out(h, o_gdn, o_dsw, o_sb, w_out, *, tm=512):
    t, d = h.shape
    return pl.pallas_call(
        _mix_out_kernel,
        grid=(t // tm,),
        in_specs=[pl.BlockSpec((tm, d), lambda i: (i, 0)),
                  pl.BlockSpec((tm, D_GDN), lambda i: (i, 0)),
                  pl.BlockSpec((tm, D_DSW), lambda i: (i, 0)),
                  pl.BlockSpec((tm, D_SB), lambda i: (i, 0)),
                  pl.BlockSpec((d, d), lambda i: (0, 0))],
        out_specs=pl.BlockSpec((tm, d), lambda i: (i, 0)),
        out_shape=jax.ShapeDtypeStruct((t, d), F32),
        compiler_params=_cparams(("arbitrary",), VMEM_LIMIT_BYTES),
        name="mix_out",
    )(h, o_gdn, o_dsw, o_sb, w_out)


def _ffn_kernel(h_ref, a_ref, b_ref, c_ref, wo_ref, nw_ref, w1_ref, w3_ref, w2_ref, o_ref, *, tf):
    h = _mixed_residual(h_ref, a_ref, b_ref, c_ref, wo_ref)
    hn = _rms(h, nw_ref[...]).astype(BF16)
    acc = h
    for c in range(w1_ref.shape[1] // tf):
        sl = slice(c * tf, (c + 1) * tf)
        a = jnp.dot(hn, w1_ref[:, sl], preferred_element_type=F32)
        b = jnp.dot(hn, w3_ref[:, sl], preferred_element_type=F32)
        g = (a * jax.nn.sigmoid(a) * b).astype(BF16)
        acc = acc + jnp.dot(g, w2_ref[sl, :], preferred_element_type=F32)
    o_ref[...] = acc


def ffn_dense(h, o_gdn, o_dsw, o_sb, w_out, nw, w1, w3, w2, *, tm=512, tf=704):
    t, d = h.shape
    f = w1.shape[1]
    const = dict(pipeline_mode=pl.Buffered(1))
    return pl.pallas_call(
        functools.partial(_ffn_kernel, tf=tf),
        grid=(t // tm,),
        in_specs=[pl.BlockSpec((tm, d), lambda i: (i, 0)),
                  pl.BlockSpec((tm, D_GDN), lambda i: (i, 0)),
                  pl.BlockSpec((tm, D_DSW), lambda i: (i, 0)),
                  pl.BlockSpec((tm, D_SB), lambda i: (i, 0)),
                  pl.BlockSpec((d, d), lambda i: (0, 0), **const),
                  pl.BlockSpec((1, d), lambda i: (0, 0)),
                  pl.BlockSpec((d, f), lambda i: (0, 0), **const),
                  pl.BlockSpec((d, f), lambda i: (0, 0), **const),
                  pl.BlockSpec((f, d), lambda i: (0, 0), **const)],
        out_specs=pl.BlockSpec((tm, d), lambda i: (i, 0)),
        out_shape=jax.ShapeDtypeStruct((t, d), F32),
        compiler_params=_cparams(("arbitrary",), VMEM_LIMIT_BYTES),
        name="ffn_dense",
    )(h, o_gdn, o_dsw, o_sb, w_out, nw.reshape(1, d), w1, w3, w2)


def _ple_kernel(*refs, moe, final):
    h_ref = refs[0]
    p_ref, proj_ref, nw_ref, gate_ref, fw_ref, o_ref = refs[-6:]
    h = h_ref[...]
    if moe:
        y0_ref, y1_ref, g_ref = refs[1:4]
        g = g_ref[...]
        h = h + g[:, 0:1] * y0_ref[...] + g[:, HEAD_DIM:HEAD_DIM + 1] * y1_ref[...]
    hn = _rms(h, nw_ref[...]).astype(BF16)
    gate = jax.nn.sigmoid(jnp.dot(hn, gate_ref[...], preferred_element_type=F32))
    emb = jnp.dot(p_ref[...].astype(BF16), proj_ref[...], preferred_element_type=F32)
    out = h + emb * gate
    if final:
        out = _rms(out, fw_ref[...])
    o_ref[...] = out


def ple_layer(h, moe_out, p, proj, nw, gate, final_w, *, final, tm=512):
    t, d = h.shape
    in_specs = [pl.BlockSpec((tm, d), lambda i: (i, 0))]
    args = [h]
    if moe_out is not None:
        y, gates = moe_out
        in_specs += [pl.BlockSpec((tm, d), lambda i: (i, 0)),
                     pl.BlockSpec((tm, d), lambda i: (t // tm + i, 0)),
                     pl.BlockSpec((tm, LANES), lambda i: (i, 0))]
        args += [y, y, gates]
    in_specs += [pl.BlockSpec((tm, PLE_DIM), lambda i: (i, 0)),
                 pl.BlockSpec((PLE_DIM, d), lambda i: (0, 0)),
                 pl.BlockSpec((1, d), lambda i: (0, 0)),
                 pl.BlockSpec((d, d), lambda i: (0, 0)),
                 pl.BlockSpec((1, d), lambda i: (0, 0))]
    args += [p, proj, nw.reshape(1, d), gate, final_w.reshape(1, d)]
    return pl.pallas_call(
        functools.partial(_ple_kernel, moe=moe_out is not None, final=final),
        grid=(t // tm,),
        in_specs=in_specs,
        out_specs=pl.BlockSpec((tm, d), lambda i: (i, 0)),
        out_shape=jax.ShapeDtypeStruct((t, d), F32),
        compiler_params=_cparams(("arbitrary",), VMEM_LIMIT_BYTES),
        name="ple_layer",
    )(*args)


def _lane_iota(shape):
    return lax.broadcasted_iota(jnp.int32, shape, len(shape) - 1)


def _head_sumsq(x):
    lo = _lane_iota(x.shape) < HEAD_DIM
    sq = x * x
    s_lo = jnp.sum(jnp.where(lo, sq, 0.0), axis=-1, keepdims=True)
    s_hi = jnp.sum(jnp.where(lo, 0.0, sq), axis=-1, keepdims=True)
    return jnp.where(lo, s_lo, s_hi)


def _head_rms(x, w2):
    return x * lax.rsqrt(_head_sumsq(x) * (1.0 / HEAD_DIM) + RMS_EPS) * w2


def _stack_heads(x):
    lo = _lane_iota(x.shape) < HEAD_DIM
    zero = jnp.zeros_like(x)
    return jnp.concatenate([jnp.where(lo, x, zero), jnp.where(lo, zero, x)], axis=0)


def _round_robin(gens):
    gens = list(gens)
    while gens:
        alive = []
        for g in gens:
            try:
                next(g)
                alive.append(g)
            except StopIteration:
                pass
        gens = alive


def _dot(a, b):
    return jnp.dot(a, b, preferred_element_type=F32)


def _dot_nt(a, b):
    return lax.dot_general(a, b, (((1,), (1,)), ((), ())), preferred_element_type=F32)


def _dot_tn(a, b):
    return lax.dot_general(a, b, (((0,), (0,)), ((), ())), preferred_element_type=F32)


def _sb_kernel(q_ref, k_ref, v_ref, nw_ref, o_ref, *, tq):
    i = pl.program_id(2)
    lane = _lane_iota((1, LANES))
    row = lax.broadcasted_iota(jnp.int32, (tq, tq), 0)
    col = lax.broadcasted_iota(jnp.int32, (tq, tq), 1)
    upper = jnp.where(row > col, 1.0, 0.0).astype(BF16)
    diag_before = col < row
    q = q_ref[0] * jnp.asarray(HEAD_DIM ** -0.5, BF16)
    zero = jnp.zeros_like(q)
    qh = (jnp.where(lane < HEAD_DIM, q, zero), jnp.where(lane < HEAD_DIM, zero, q))

    def body(carry):
        jj, _, heads = carry
        j = i - jj
        start = pl.multiple_of(j * tq, tq)
        kj = k_ref[0, pl.ds(start, tq), :]
        vj = v_ref[0, pl.ds(start, tq), :]
        before = jnp.logical_or(diag_before, jj > 0)
        out = [None, None]

        def head_chain(hh):
            acc, c = heads[hh]
            yield
            z = _dot_nt(qh[hh], kj)
            sp = jnp.maximum(z, 0.0) + jnp.log(1.0 + jnp.exp(-jnp.abs(z)))
            lk = jnp.where(before, -sp, 0.0)
            lk_hi = lk.astype(BF16)
            lk_lo = (lk - lk_hi.astype(F32)).astype(BF16)
            yield
            suffix = _dot(lk_hi, upper) + _dot(lk_lo, upper)
            a = jnp.where(before, jnp.exp(z - sp + suffix + c), 0.0).astype(BF16)
            yield
            out[hh] = (acc + _dot(a, vj), c + jnp.sum(lk, axis=-1, keepdims=True))

        _round_robin([head_chain(hh) for hh in range(2)])
        cmax = jnp.maximum(jnp.max(out[0][1]), jnp.max(out[1][1]))
        return jj + 1, cmax, tuple(out)

    def cond(carry):
        jj, cmax, _ = carry
        return jnp.logical_and(jj <= i, cmax > SB_EXIT_LOG)

    init = tuple((jnp.zeros((tq, LANES), F32), jnp.zeros((tq, 1), F32)) for _ in range(2))
    _, _, ((acc0, _), (acc1, _)) = lax.while_loop(cond, body, (jnp.int32(0), jnp.float32(0.0), init))
    o = jnp.where(lane < HEAD_DIM, acc0, acc1)
    o_ref[0] = _head_rms(o, nw_ref[...]).astype(o_ref.dtype)


def sb_attention(proj, norm_w, *, tq=256):
    b, s, _ = proj.shape
    cq, ck, cv = COL_SQ // LANES, COL_SK // LANES, COL_SV // LANES
    nw2 = jnp.tile(norm_w.astype(F32), 2).reshape(1, LANES)
    return pl.pallas_call(
        functools.partial(_sb_kernel, tq=tq),
        grid=(b, N_PAIRS_SB, s // tq),
        in_specs=[pl.BlockSpec((1, tq, LANES), lambda bi, p, i: (bi, i, cq + p)),
                  pl.BlockSpec((1, s, LANES), lambda bi, p, i: (bi, 0, ck + p)),
                  pl.BlockSpec((1, s, LANES), lambda bi, p, i: (bi, 0, cv + p)),
                  pl.BlockSpec((1, LANES), lambda bi, p, i: (0, 0))],
        out_specs=pl.BlockSpec((1, tq, LANES), lambda bi, p, i: (bi, i, p)),
        out_shape=jax.ShapeDtypeStruct((b, s, D_SB), BF16),
        compiler_params=_cparams(("arbitrary",) * 3, VMEM_LIMIT_BYTES),
        name="sb_attention",
    )(proj, proj, proj, nw2)


DSW_PERM_ROWS = 1024


def _dsw_kernel(slope_ref, q_ref, k_ref, v_ref, dist_ref, nw_ref, o_ref,
                qf, kf, vf, qp, kp, vp, op, lp, oacc, lacc, *, tq):
    hp = pl.program_id(1)
    seq = q_ref.shape[1]
    lane = _lane_iota((1, LANES))
    lo = lane < HEAD_DIM
    nperm = seq // DSW_PERM_ROWS

    def to_f32(m, _):
        rows = pl.ds(pl.multiple_of(m * DSW_PERM_ROWS, DSW_PERM_ROWS), DSW_PERM_ROWS)
        for src, dst in ((q_ref, qf), (k_ref, kf), (v_ref, vf)):
            dst[rows, :] = src[0, rows, :].astype(F32)
        return 0

    lax.fori_loop(0, nperm, to_f32, 0)

    def attend(qs, ks, vs, o_dst, l_dst, r):
        length = seq // r
        tb = min(tq, length)
        nblk = length // tb
        slopes = [slope_ref[2 * hp + hh] * float(r) for hh in range(2)]

        def head_chain(hh, q, kc, vc, dist, res):
            zero = jnp.zeros_like(q)
            qh = jnp.where(lo, q, zero) if hh == 0 else jnp.where(lo, zero, q)
            yield
            sc = _dot_nt(qh, kc) - slopes[hh] * dist
            m = jnp.max(sc, axis=-1, keepdims=True)
            e = jnp.exp(sc - m)
            den = jnp.sum(e, axis=-1, keepdims=True)
            eb = e.astype(BF16)
            yield
            res[hh] = (_dot(eb, vc) / den, m + jnp.log(den))

        def block_chains(it):
            n = it % nblk
            start = pl.multiple_of(it * tb, tb)
            pstart = pl.multiple_of(jnp.maximum(start - DSW_SPAN, 0), DSW_SPAN)
            q = qs[pl.ds(start, tb), :] * jnp.asarray(HEAD_DIM ** -0.5, BF16)
            kc = jnp.concatenate([ks[pl.ds(pstart, DSW_SPAN), :], ks[pl.ds(start, tb), :]], axis=0)
            vc = jnp.concatenate([vs[pl.ds(pstart, DSW_SPAN), :], vs[pl.ds(start, tb), :]], axis=0)
            dist = dist_ref[jnp.minimum(n, 1)]
            res = [None, None]
            return start, res, [head_chain(hh, q, kc, vc, dist, res) for hh in range(2)]

        def body(it2, _):
            blocks = [block_chains(2 * it2 + u) for u in range(2)]
            _round_robin([g for _, _, gens in blocks for g in gens])
            for start, res, _ in blocks:
                o_dst[pl.ds(start, tb), :] = jnp.where(lo, res[0][0], res[1][0])
                l_dst[pl.ds(start, tb), :] = jnp.where(lo, res[0][1], res[1][1])
            return 0

        lax.fori_loop(0, r * nblk // 2, body, 0)

    for _, r in DSW_CONFIGS:
        if r == 1:
            attend(q_ref.at[0], k_ref.at[0], v_ref.at[0], oacc, lacc, 1)
            continue
        length = seq // r
        per = DSW_PERM_ROWS // r

        def permute(m, _, r=r, length=length, per=per):
            base = pl.multiple_of(m * DSW_PERM_ROWS, DSW_PERM_ROWS)
            for c in range(r):
                dst = pl.ds(pl.multiple_of(c * length + m * per, per), per)
                for src, dstref in ((qf, qp), (kf, kp), (vf, vp)):
                    dstref[dst, :] = src.at[pl.ds(base, DSW_PERM_ROWS)][pl.ds(c, per, stride=r), :].astype(BF16)
            return 0

        lax.fori_loop(0, nperm, permute, 0)
        attend(qp, kp, vp, op, lp, r)

        def merge(m, _, r=r, length=length, per=per):
            base = pl.multiple_of(m * DSW_PERM_ROWS, DSW_PERM_ROWS)
            for c in range(r):
                src = pl.ds(pl.multiple_of(c * length + m * per, per), per)
                rows = pl.ds(c, per, stride=r)
                o_view = oacc.at[pl.ds(base, DSW_PERM_ROWS)]
                l_view = lacc.at[pl.ds(base, DSW_PERM_ROWS)]
                o_old, l_old = o_view[rows, :], l_view[rows, :]
                o_new, l_new = op[src, :], lp[src, :]
                l_max = jnp.maximum(l_old, l_new)
                w_old, w_new = jnp.exp(l_old - l_max), jnp.exp(l_new - l_max)
                tot = w_old + w_new
                o_view[rows, :] = (w_old * o_old + w_new * o_new) / tot
                l_view[rows, :] = l_max + jnp.log(tot)
            return 0

        lax.fori_loop(0, nperm, merge, 0)

    def finish(m, _):
        rows = pl.ds(pl.multiple_of(m * DSW_PERM_ROWS, DSW_PERM_ROWS), DSW_PERM_ROWS)
        o_ref[0, rows, :] = _head_rms(oacc[rows, :], nw_ref[...]).astype(o_ref.dtype)
        return 0

    lax.fori_loop(0, nperm, finish, 0)


def _dsw_dist_table(tq):
    qi = np.arange(tq)[:, None]
    ki = np.arange(tq + DSW_SPAN)[None, :]
    dist = qi - ki + DSW_SPAN
    valid = (dist >= 0) & (dist <= DSW_SPAN)
    later = np.where(valid, dist, 1e30).astype(np.float32)
    first = np.where(valid & (ki >= DSW_SPAN), dist, 1e30).astype(np.float32)
    return jnp.asarray(np.stack([first, later]))


def dsw_mixer(proj, slopes, norm_w, *, tq=256):
    b, s, _ = proj.shape
    cq, ck, cv = COL_DQ // LANES, COL_DK // LANES, COL_DV // LANES
    nw2 = jnp.tile(norm_w.astype(F32), 2).reshape(1, LANES)
    dist = _dsw_dist_table(tq)

    def col(base):
        return pl.BlockSpec((1, s, LANES), lambda bi, p: (bi, 0, base + p))

    return pl.pallas_call(
        functools.partial(_dsw_kernel, tq=tq),
        grid=(b, N_PAIRS_DSW),
        in_specs=[pl.BlockSpec(memory_space=pltpu.SMEM),
                  col(cq), col(ck), col(cv),
                  pl.BlockSpec((2, tq, tq + DSW_SPAN), lambda bi, p: (0, 0, 0)),
                  pl.BlockSpec((1, LANES), lambda bi, p: (0, 0))],
        out_specs=pl.BlockSpec((1, s, LANES), lambda bi, p: (bi, 0, p)),
        out_shape=jax.ShapeDtypeStruct((b, s, D_DSW), BF16),
        scratch_shapes=[pltpu.VMEM((s, LANES), F32)] * 3
        + [pltpu.VMEM((s, LANES), BF16)] * 3
        + [pltpu.VMEM((s, LANES), F32)] * 4,
        compiler_params=_cparams(("arbitrary",) * 2, VMEM_LIMIT_BYTES),
        name="dsw_mixer",
    )(slopes, proj, proj, proj, dist, nw2)


def _gdn_kernel(xq_ref, xk_ref, xv_ref, z_ref, ab_ref, cw_ref, expand_ref, alog_ref, dtb_ref, nw_ref, o_ref,
                m_s, n_s, a_s, b_s, gl_s, *, group):
    c = GDN_CHUNK
    seq = xq_ref.shape[1]
    nchunk = seq // c
    lane = _lane_iota((c, LANES))
    rowi = lax.broadcasted_iota(jnp.int32, (c, LANES), 0)
    colj = jnp.where(lane < HEAD_DIM, lane, lane - HEAD_DIM)
    causal = rowi >= colj
    strict = rowi > colj
    eye_cat = jnp.where(rowi == colj, 1.0, 0.0)
    r2 = lax.broadcasted_iota(jnp.int32, (LANES, LANES), 0)
    c2 = lax.broadcasted_iota(jnp.int32, (LANES, LANES), 1)
    same_head = (r2 < HEAD_DIM) == (c2 < HEAD_DIM)
    nw2 = nw_ref[...]

    def conv_silu(x_ref, which, p, start, has_prev):
        lanes = slice(p * LANES, (p + 1) * LANES)
        w = cw_ref[:, which * D_GDN + p * LANES:which * D_GDN + (p + 1) * LANES]
        x = x_ref[0, pl.ds(start, c), lanes].astype(F32)
        pstart = pl.multiple_of(jnp.maximum(start - 16, 0), 16)
        prev = x_ref[0, pl.ds(pstart, 16), lanes].astype(F32)[8:]
        xx = jnp.concatenate([jnp.where(has_prev, prev, 0.0), x], axis=0)
        y = xx[8:] * w[3:4]
        for s in (1, 2, 3):
            y = y + pltpu.roll(xx, s, axis=0)[8:] * w[3 - s:4 - s]
        return y * jax.nn.sigmoid(y)

    def chunk_local(p, n):
        start = pl.multiple_of(n * c, c)
        has_prev = n > 0
        q = conv_silu(xq_ref, 0, p, start, has_prev)
        k = conv_silu(xk_ref, 1, p, start, has_prev)
        v = conv_silu(xv_ref, 2, p, start, has_prev)
        q = q * lax.rsqrt(_head_sumsq(q) + RMS_EPS) * (HEAD_DIM ** -0.5)
        k = k * lax.rsqrt(_head_sumsq(k) + RMS_EPS)
        ab = _dot(ab_ref[0, pl.ds(start, c), :], expand_ref[p])
        a_e, b_e = ab[:, :LANES], ab[:, LANES:]
        xg = a_e + dtb_ref[p]
        g = -jnp.exp(alog_ref[p]) * (jnp.maximum(xg, 0.0) + jnp.log(1.0 + jnp.exp(-jnp.abs(xg))))
        beta = jax.nn.sigmoid(b_e)
        gcol = g
        for s in (1, 2, 4, 8, 16, 32):
            gcol = gcol + jnp.where(rowi >= s, pltpu.roll(gcol, s, axis=0), 0.0)
        grow = jnp.sum(jnp.where(rowi == colj, gcol, 0.0), axis=0, keepdims=True)
        glast = gcol[c - 1:c, :]
        diff = gcol - grow
        decay = jnp.exp(jnp.where(causal, diff, NEG_BIG))
        eg = jnp.exp(gcol)
        kb = k.astype(BF16)
        ks = _stack_heads(kb)
        lhs = jnp.concatenate([kb, q.astype(BF16)], axis=0)
        rhs = jnp.concatenate([_stack_heads((v * beta).astype(BF16)),
                               _stack_heads((k * beta * eg).astype(BF16))], axis=1)
        yield
        kq = _dot_nt(lhs, ks)
        kk_cat, qk_cat = kq[:c], kq[c:]
        pm = jnp.where(strict, -(beta * kk_cat * decay), 0.0)
        t = eye_cat + pm
        pb = pm.astype(BF16)
        pbs = _stack_heads(pb)
        yield
        pm = _dot(pb, pbs)
        for _ in range(4):
            pb = pm.astype(BF16)
            lhs = jnp.concatenate([t.astype(BF16), pb], axis=0)
            pbs = _stack_heads(pb)
            yield
            res = _dot(lhs, pbs)
            t = t + res[:c]
            pm = res[c:]
        pb = pm.astype(BF16)
        tb = t.astype(BF16)
        pbs = _stack_heads(pb)
        yield
        t = t + _dot(tb, pbs)
        tb = t.astype(BF16)
        yield
        uw = _dot(tb, rhs)
        wub = uw.astype(BF16)
        kdb = (k * jnp.exp(glast - gcol)).astype(BF16)
        qkb = jnp.where(causal, qk_cat * decay, 0.0).astype(BF16)
        wus = jnp.concatenate([_stack_heads(wub[:, :LANES]), _stack_heads(wub[:, LANES:])], axis=1)
        yield
        kuw = _dot_tn(kdb, wub)
        yield
        quw = _dot(qkb, wus)
        rows = pl.ds(start, c)
        m_s[p, n] = jnp.where(same_head, -kuw[:, LANES:], 0.0).astype(BF16)
        n_s[p, n] = jnp.where(same_head, kuw[:, :LANES], 0.0).astype(BF16)
        a_s[p, rows, :] = (q * eg - quw[:, LANES:]).astype(BF16)
        b_s[p, rows, :] = quw[:, :LANES].astype(BF16)
        gl_s[p, n] = jnp.broadcast_to(jnp.exp(glast), (8, LANES))

    for p in range(N_PAIRS_GDN):
        def local_body(m, _, p=p):
            _round_robin([chunk_local(p, m * group + gi) for gi in range(group)])
            return 0
        lax.fori_loop(0, nchunk // group, local_body, 0)

    def state_step(p, n, state, result):
        start = pl.multiple_of(n * c, c)
        rows = pl.ds(start, c)
        lanes = slice(p * LANES, (p + 1) * LANES)
        lhs = jnp.concatenate([m_s[p, n], a_s[p, rows, :]], axis=0)
        sb = state.astype(BF16)
        yield
        res = _dot(lhs, sb)
        result[p] = state * gl_s[p, n][0:1] + res[:LANES] + n_s[p, n].astype(F32)
        o = res[LANES:] + b_s[p, rows, :].astype(F32)
        zg = z_ref[0, rows, lanes].astype(F32)
        out = _head_rms(o, nw2) * (zg * jax.nn.sigmoid(zg))
        o_ref[0, rows, lanes] = out.astype(o_ref.dtype)

    def state_body(n, states):
        result = [None] * N_PAIRS_GDN
        _round_robin([state_step(p, n, states[p], result) for p in range(N_PAIRS_GDN)])
        return tuple(result)

    lax.fori_loop(0, nchunk, state_body, tuple(jnp.zeros((LANES, LANES), F32) for _ in range(N_PAIRS_GDN)))


def gdn_mixer(proj, conv_w, a_log, dt_bias, norm_w):
    b, s, _ = proj.shape
    npair = N_PAIRS_GDN
    ex = np.zeros((npair, LANES, 2 * LANES), np.float32)
    for p in range(npair):
        for hh in range(2):
            ex[p, 2 * p + hh, hh * HEAD_DIM:(hh + 1) * HEAD_DIM] = 1.0
            ex[p, 2 * npair + 2 * p + hh, LANES + hh * HEAD_DIM:LANES + (hh + 1) * HEAD_DIM] = 1.0
    expand = jnp.asarray(ex, BF16)
    alog2 = jnp.repeat(a_log.astype(F32).reshape(npair, 1, 2), HEAD_DIM, axis=2)
    dtb2 = jnp.repeat(dt_bias.astype(F32).reshape(npair, 1, 2), HEAD_DIM, axis=2)
    nw2 = jnp.tile(norm_w.astype(F32), 2).reshape(1, LANES)
    cw = conv_w.astype(F32)
    once = dict(pipeline_mode=pl.Buffered(1))

    def xcol(base):
        return pl.BlockSpec((1, s, D_GDN), lambda bi: (bi, 0, base // D_GDN), **once)

    def whole(shape):
        return pl.BlockSpec(shape, lambda bi: (0,) * len(shape))

    return pl.pallas_call(
        functools.partial(_gdn_kernel, group=8),
        grid=(b,),
        in_specs=[xcol(COL_GQ), xcol(COL_GK), xcol(COL_GV), xcol(COL_GZ),
                  pl.BlockSpec((1, s, LANES), lambda bi: (bi, 0, COL_AB // LANES), **once),
                  whole((CONV_WIDTH, 3 * D_GDN)),
                  whole((npair, LANES, 2 * LANES)),
                  whole((npair, 1, LANES)),
                  whole((npair, 1, LANES)),
                  whole((1, LANES))],
        out_specs=pl.BlockSpec((1, s, D_GDN), lambda bi: (bi, 0, 0)),
        out_shape=jax.ShapeDtypeStruct((b, s, D_GDN), BF16),
        scratch_shapes=[pltpu.VMEM((npair, s // GDN_CHUNK, LANES, LANES), BF16)] * 2
        + [pltpu.VMEM((npair, s, LANES), BF16)] * 2
        + [pltpu.VMEM((npair, s // GDN_CHUNK, 8, LANES), F32)],
        compiler_params=_cparams(("arbitrary",), VMEM_LIMIT_BYTES),
        name="gdn_mixer",
    )(proj, proj, proj, proj, proj, cw, expand, alog2, dtb2, nw2)


def _router_kernel(h_ref, nw_ref, rw_ref, hn_ref, idx_ref, gate_ref):
    hn = _rms(h_ref[...], nw_ref[...])
    hn_ref[...] = hn
    lt = lax.dot_general(rw_ref[...], hn, (((1,), (1,)), ((), ())),
                         preferred_element_type=F32, precision=lax.Precision.HIGHEST)
    row = lax.broadcasted_iota(jnp.int32, lt.shape, 0)
    v1 = jnp.max(lt, axis=0, keepdims=True)
    i1 = jnp.min(jnp.where(lt == v1, row, N_EXPERTS), axis=0, keepdims=True)
    lt2 = jnp.where(row == i1, -jnp.inf, lt)
    v2 = jnp.max(lt2, axis=0, keepdims=True)
    i2 = jnp.min(jnp.where(lt2 == v2, row, N_EXPERTS), axis=0, keepdims=True)
    e = jnp.exp(v2 - v1)
    g1 = 1.0 / (1.0 + e)
    idx_ref[...] = jnp.concatenate([i1, i2], axis=0)
    g8 = jnp.concatenate([g1, e * g1, jnp.zeros((N_EXPERTS - 2, g1.shape[1]), F32)], axis=0)
    srow = lax.broadcasted_iota(jnp.int32, (N_EXPERTS, LANES), 0)
    slane = lax.broadcasted_iota(jnp.int32, (N_EXPERTS, LANES), 1)
    sel = jnp.where(srow == slane // HEAD_DIM, 1.0, 0.0)
    gate_ref[...] = lax.dot_general(g8, sel, (((0,), (0,)), ((), ())),
                                    preferred_element_type=F32, precision=lax.Precision.HIGHEST)


def moe_router(h, nw, router_w, *, tm=512):
    t, d = h.shape
    return pl.pallas_call(
        _router_kernel,
        grid=(t // tm,),
        in_specs=[pl.BlockSpec((tm, d), lambda i: (i, 0)),
                  pl.BlockSpec((1, d), lambda i: (0, 0)),
                  pl.BlockSpec((N_EXPERTS, d), lambda i: (0, 0))],
        out_specs=[pl.BlockSpec((tm, d), lambda i: (i, 0)),
                   pl.BlockSpec((2, tm), lambda i: (0, i)),
                   pl.BlockSpec((tm, LANES), lambda i: (i, 0))],
        out_shape=[jax.ShapeDtypeStruct((t, d), F32),
                   jax.ShapeDtypeStruct((2, t), jnp.int32),
                   jax.ShapeDtypeStruct((t, LANES), F32)],
        compiler_params=_cparams(("arbitrary",), VMEM_LIMIT_BYTES),
        name="moe_router",
    )(h, nw.reshape(1, d), router_w.astype(F32).T)


GATHER_DMA_PRIORITY = 1
SCATTER_DMA_PRIORITY = 0


def _moe_kernel(te_ref, tv_ref, tok_ref, tokn_ref, dstp_ref, dst_ref, hn_hbm, w1_ref, w3_ref, w2_ref, y_hbm,
                xbuf, xb, acc, ybuf, gsem, ssem, *, tm, nf):
    i = pl.program_id(0)
    f = pl.program_id(1)
    nt = pl.num_programs(0)
    valid = tv_ref[i] > 0
    chunk = tm // (nf - 1)

    def gather_row(tref, r):
        return pltpu.make_async_copy(hn_hbm.at[pl.ds(tref[0, 0, r], 1)], xbuf.at[pl.ds(r, 1)], gsem)

    def scatter_row(dref, r):
        return pltpu.make_async_copy(ybuf.at[pl.ds(r, 1)], y_hbm.at[pl.ds(dref[0, 0, r], 1)], ssem)

    def gather_wait():
        pltpu.make_async_copy(xbuf, xbuf, gsem).wait()

    def scatter_wait():
        pltpu.make_async_copy(ybuf, ybuf, ssem).wait()

    def issue_chunk():
        base = f * chunk
        for k in range(chunk):
            gather_row(tokn_ref, base + k).start(priority=GATHER_DMA_PRIORITY)
            scatter_row(dstp_ref, base + k).start(priority=SCATTER_DMA_PRIORITY)

    def expert_slice():
        x = xb[...]
        a = _dot(x, w1_ref[0, 0])
        b = _dot(x, w3_ref[0, 0])
        g = (a * jax.nn.sigmoid(a) * b).astype(BF16)
        acc[...] += _dot(g, w2_ref[0])

    @pl.when(jnp.logical_and(i == 0, f == 0))
    def _():
        ybuf[...] = jnp.zeros_like(ybuf)
        tail = pltpu.make_async_copy(ybuf, y_hbm.at[pl.ds(y_hbm.shape[0] - tm, tm)], ssem)
        tail.start()
        tail.wait()

        def body(r, _):
            gather_row(tok_ref, r).start()
            return 0
        lax.fori_loop(0, tm, body, 0, unroll=8)

    @pl.when(f == 0)
    def _():
        gather_wait()
        xb[...] = xbuf[...].astype(BF16)
        acc[...] = jnp.zeros_like(acc)

    @pl.when(jnp.logical_and(valid, f < nf - 1))
    def _():
        expert_slice()
        issue_chunk()

    @pl.when(jnp.logical_and(valid, f == nf - 1))
    def _():
        expert_slice()

    @pl.when(jnp.logical_and(jnp.logical_not(valid), f < nf - 1))
    def _():
        issue_chunk()

    @pl.when(f == nf - 1)
    def _():
        scatter_wait()
        ybuf[...] = acc[...]

    @pl.when(jnp.logical_and(i == nt - 1, f == nf - 1))
    def _():
        def body(r, _):
            scatter_row(dst_ref, r).start()
            return 0
        lax.fori_loop(0, tm, body, 0, unroll=8)
        scatter_wait()
        gather_wait()


def moe_experts(hn, idx, w1, w3, w2, *, tm=768):
    t, d = hn.shape
    nf, tf = w1.shape[1], w1.shape[3]
    assert tm % (nf - 1) == 0 and (tm // (nf - 1)) % 8 == 0
    na = 2 * t
    nt = -(-na // tm) + N_EXPERTS
    rmax = nt * tm
    e_flat = idx.reshape(na)
    onehot = (e_flat[:, None] == jnp.arange(N_EXPERTS)[None, :]).astype(jnp.int32)
    csum = jnp.cumsum(onehot, axis=0)
    counts = csum[-1]
    rank = jnp.sum(csum * onehot, axis=1) - 1
    padded = ((counts + tm - 1) // tm) * tm
    ends = jnp.cumsum(padded)
    offs = ends - padded
    dest = offs[e_flat] + rank
    a_ids = jnp.arange(na, dtype=jnp.int32)
    row_asg = jnp.full((rmax,), -1, jnp.int32).at[dest].set(a_ids, unique_indices=True)
    row_real = row_asg >= 0
    row_tok = jnp.where(row_real, row_asg % t, 0)
    row_dst = jnp.where(row_real, row_asg, na + jnp.arange(rmax, dtype=jnp.int32) % tm)
    tile_start = jnp.arange(nt, dtype=jnp.int32) * tm
    tile_valid = (tile_start < ends[-1]).astype(jnp.int32)
    tile_exp = jnp.minimum(jnp.sum((tile_start[:, None] >= ends[None, :]).astype(jnp.int32), axis=1),
                           N_EXPERTS - 1).astype(jnp.int32)
    last_e = tile_exp[jnp.maximum(ends[-1] // tm - 1, 0)]
    tile_exp = jnp.where(tile_valid > 0, tile_exp, last_e)
    row_tok = row_tok.reshape(nt, 1, tm)
    row_dst = row_dst.reshape(nt, 1, tm)

    def frozen_f(i, f, tv):
        return jnp.where(tv[i] > 0, f, nf - 1)

    def rows_of(tile_of):
        return pl.BlockSpec((1, 1, tm), lambda i, f, te, tv: (tile_of(i), 0, 0), memory_space=pltpu.SMEM)

    grid_spec = pltpu.PrefetchScalarGridSpec(
        num_scalar_prefetch=2,
        grid=(nt, nf),
        in_specs=[rows_of(lambda i: i),
                  rows_of(lambda i: jnp.minimum(i + 1, nt - 1)),
                  rows_of(lambda i: jnp.maximum(i - 1, 0)),
                  rows_of(lambda i: i),
                  pl.BlockSpec(memory_space=pl.ANY),
                  pl.BlockSpec((1, 1, d, tf), lambda i, f, te, tv: (te[i], frozen_f(i, f, tv), 0, 0)),
                  pl.BlockSpec((1, 1, d, tf), lambda i, f, te, tv: (te[i], frozen_f(i, f, tv), 0, 0)),
                  pl.BlockSpec((1, tf, d), lambda i, f, te, tv: (te[i], frozen_f(i, f, tv), 0))],
        out_specs=pl.BlockSpec(memory_space=pl.ANY),
        scratch_shapes=[pltpu.VMEM((tm, d), F32),
                        pltpu.VMEM((tm, d), BF16),
                        pltpu.VMEM((tm, d), F32),
                        pltpu.VMEM((tm, d), F32),
                        pltpu.SemaphoreType.DMA(()),
                        pltpu.SemaphoreType.DMA(())],
    )
    return pl.pallas_call(
        functools.partial(_moe_kernel, tm=tm, nf=nf),
        grid_spec=grid_spec,
        out_shape=jax.ShapeDtypeStruct((na + tm, d), F32),
        compiler_params=_cparams(("arbitrary",) * 2, VMEM_LIMIT_BYTES),
        name="moe_experts",
    )(tile_exp, tile_valid, row_tok, row_tok, row_dst, row_dst, hn, w1, w3, w2)


MOE_TF = 512


def _tile_expert_cols(w):
    e, d, ff = w.shape
    return w.astype(BF16).reshape(e, d, ff // MOE_TF, MOE_TF).transpose(0, 2, 1, 3)


def moe_layer(h, nw, router_w, w1, w3, w2):
    hn, idx, gates = moe_router(h, nw, router_w)
    return moe_experts(hn, idx, _tile_expert_cols(w1), _tile_expert_cols(w3), w2.astype(BF16)), gates


def _pack_w_in(w):
    d = w.shape[0]
    n_ab = 2 * N_PAIRS_GDN * 2
    gdn_main = 4 * D_GDN
    rest = w[:, gdn_main + n_ab:]
    ab = jnp.pad(w[:, gdn_main:gdn_main + n_ab], ((0, 0), (0, LANES - n_ab)))
    return jnp.concatenate([w[:, :gdn_main], rest, ab], axis=1).astype(BF16)


def kernel(x, p, attn_norm_w, w_in, conv_w, a_log, dt_bias, gdn_norm_w, dsw_norm_w, sb_norm_w, w_out,
           ffn_norm_w, ffn_w1, ffn_w3, ffn_w2, router_w, moe_w1, moe_w3, moe_w2, ple_proj, ple_norm_w,
           ple_gate, final_norm_w):
    b, s, d = x.shape
    depth = w_in.shape[0]
    t = b * s
    h = x.reshape(t, d)
    n_dsw_heads = 2 * N_PAIRS_DSW
    slopes = jnp.asarray([2.0 ** (-8.0 * (i + 1) / n_dsw_heads) for i in range(n_dsw_heads)], F32)
    for i in range(depth):
        proj = norm_proj(h, attn_norm_w[i], _pack_w_in(w_in[i])).reshape(b, s, D_PROJ)
        o_gdn = gdn_mixer(proj, conv_w[i], a_log[i], dt_bias[i], gdn_norm_w[i]).reshape(t, D_GDN)
        o_dsw = dsw_mixer(proj, slopes, dsw_norm_w[i]).reshape(t, D_DSW)
        o_sb = sb_attention(proj, sb_norm_w[i]).reshape(t, D_SB)
        j = i // 2
        if i % 2 == 0:
            h = ffn_dense(h, o_gdn, o_dsw, o_sb, w_out[i].astype(BF16), ffn_norm_w[i],
                          ffn_w1[j].astype(BF16), ffn_w3[j].astype(BF16), ffn_w2[j].astype(BF16))
            moe_out = None
        else:
            h = mix_out(h, o_gdn, o_dsw, o_sb, w_out[i].astype(BF16))
            moe_out = moe_layer(h, ffn_norm_w[i], router_w[j], moe_w1[j], moe_w3[j], moe_w2[j])
        h = ple_layer(h, moe_out, p[i].reshape(t, PLE_DIM), ple_proj[i].astype(BF16), ple_norm_w[i],
                      ple_gate[i].astype(BF16), final_norm_w, final=(i == depth - 1))
    return h.reshape(b, s, d)
```
